```python
import math
import jax, jax.numpy as jnp
from jax import lax
import numpy as np

D_MODEL = 1024
BATCH = 2
SEQ = 8192
DEPTH = 2

CTX_LEN = 256
GRID_W = 64
MIX_W = 2 * D_MODEL
SSD_W = MIX_W // 2
HY_W = MIX_W - SSD_W
SSD_HEADDIM = 64
SSD_HEADS = SSD_W // SSD_HEADDIM
SSD_GROUPS = 2
SSD_STATE = 128
SSD_CONV = 3
CHUNK = 128
HY_CONV = 3
HY_EMB = 33
HY_BANDS = (HY_EMB - 1) // 2
HY_FILT_W = 64
HY_TARGET = 1e-2
HY_FAST_PCT = 0.3
HY_SLOW_PCT = 1.5
HY_MIN_DECAY = math.log(HY_TARGET) / HY_SLOW_PCT
HY_MAX_DECAY = math.log(HY_TARGET) / HY_FAST_PCT
EPS = 1e-6

XBC_W = SSD_W + 2 * SSD_GROUPS * SSD_STATE
COL_ZS = 0
COL_XBC = COL_ZS + SSD_W
COL_DT = COL_XBC + XBC_W
COL_ZH = COL_DT + 2 * SSD_HEADS
COL_HY = COL_ZH + HY_W
IN_W = COL_HY + 3 * HY_W

kernel_name = 'hymba_ssd_hyena_prefix_dit'


def rmsnorm(u, w):
    uf = u.astype(jnp.float32)
    uf = uf * lax.rsqrt(jnp.mean(uf * uf, axis=-1, keepdims=True) + EPS)
    return uf.astype(u.dtype) * w


def seq_conv(u, w, b):
    k = w.shape[0]
    out = lax.conv_general_dilated(
        u, w[:, None, :].astype(u.dtype), window_strides=(1,),
        padding=[(k // 2, k // 2)], dimension_numbers=('NWC', 'WIO', 'NWC'),
        feature_group_count=u.shape[-1])
    return out + b


def grid_row_conv(u, w, b):
    bsz, n_tok, ch = u.shape
    rows = n_tok // GRID_W
    return seq_conv(u.reshape(bsz * rows, GRID_W, ch), w, b).reshape(bsz, n_tok, ch)


def ssd_chunked(xs, dt, A, Bm, Cm, h0):
    b, L, H, P = xs.shape
    G, N = Bm.shape[2], Bm.shape[3]
    Hg = H // G
    nc = L // CHUNK
    x_c = xs.reshape(b, nc, CHUNK, G, Hg, P)
    dt_c = dt.reshape(b, nc, CHUNK, G, Hg)
    B_c = Bm.reshape(b, nc, CHUNK, G, N)
    C_c = Cm.reshape(b, nc, CHUNK, G, N)
    a_cum = jnp.cumsum(dt_c * A.reshape(G, Hg), axis=2)
    lower = jnp.tril(jnp.ones((CHUNK, CHUNK), dtype=bool))
    seg = a_cum[:, :, :, None] - a_cum[:, :, None, :]
    decay = jnp.exp(jnp.where(lower[:, :, None, None], seg, -jnp.inf))
    xdt = dt_c[..., None] * x_c
    cb = jnp.einsum('bcign,bcjgn->bcijg', C_c, B_c)
    y_intra = jnp.einsum('bcijg,bcijgh,bcjghp->bcighp', cb, decay, xdt)
    to_end = jnp.exp(a_cum[:, :, -1:] - a_cum)
    states = jnp.einsum('bcjgn,bcjghp->bcghpn', B_c, to_end[..., None] * xdt)
    chunk_decay = jnp.exp(a_cum[:, :, -1])

    def carry(h, inp):
        s_c, d_c = inp
        return d_c[..., None, None] * h + s_c, h

    h_last, h_prev = lax.scan(carry, h0, (jnp.moveaxis(states, 1, 0),
                                          jnp.moveaxis(chunk_decay, 1, 0)))
    h_prev = jnp.moveaxis(h_prev, 0, 1)
    y_inter = jnp.einsum('bcign,bcigh,bcghpn->bcighp', C_c, jnp.exp(a_cum), h_prev)
    return (y_intra + y_inter).reshape(b, L, H, P), h_last


def ssd_direction(xs, dt_raw, Bm, Cm, dt_bias, a_log, d_skip, h0):
    dt = jax.nn.softplus(dt_raw.astype(jnp.float32) + dt_bias)
    A = -jnp.exp(a_log.astype(jnp.float32))
    y, h = ssd_chunked(xs, dt, A, Bm, Cm, h0)
    return y + d_skip[:, None] * xs, h


def ssd_mixer(xbc_raw, dt_raw, conv_fn, conv_w, conv_b, dt_bias, a_log, d_skip, h0_f, h0_b):
    xbc = jax.nn.silu(conv_fn(xbc_raw, conv_w, conv_b))
    bsz, L, _ = xbc.shape
    gn = SSD_GROUPS * SSD_STATE
    xs = xbc[..., :SSD_W].reshape(bsz, L, SSD_HEADS, SSD_HEADDIM)
    Bm = xbc[..., SSD_W:SSD_W + gn].reshape(bsz, L, SSD_GROUPS, SSD_STATE)
    Cm = xbc[..., SSD_W + gn:].reshape(bsz, L, SSD_GROUPS, SSD_STATE)
    rev = lambda t: jnp.flip(t, axis=1)
    y_f, h_f = ssd_direction(xs, dt_raw[..., :SSD_HEADS], Bm, Cm,
                             dt_bias[0], a_log[0], d_skip[0], h0_f)
    y_b, h_b = ssd_direction(rev(xs), rev(dt_raw[..., SSD_HEADS:]), rev(Bm), rev(Cm),
                             dt_bias[1], a_log[1], d_skip[1], h0_b)
    y = (y_f + rev(y_b)).reshape(bsz, L, SSD_W)
    return y, h_f, h_b


def hyena_filters(L, w1, b1, w2, b2, w3, b3, w4, freq):
    t = jnp.linspace(0.0, 1.0, L, dtype=jnp.float32)[:, None]
    w = (2.0 * math.pi / L) * jnp.arange(L, dtype=jnp.float32)[:, None]
    f = jnp.linspace(1e-4, HY_BANDS - 1, HY_BANDS, dtype=jnp.float32)[None, :]
    feats = jnp.concatenate([t, jnp.cos(f * w), -jnp.sin(f * w)], axis=-1)
    fr = freq.astype(jnp.float32)
    h = jnp.sin(fr * (feats @ w1 + b1))
    h = jnp.sin(fr * (h @ w2 + b2))
    h = jnp.sin(fr * (h @ w3 + b3))
    h = (h @ w4).astype(jnp.float32)
    deltas = jnp.abs(jnp.linspace(HY_MIN_DECAY, HY_MAX_DECAY, HY_W, dtype=jnp.float32))
    window = jnp.exp(-t * deltas)
    return h[:, :HY_W] * window, h[:, HY_W:] * window


def bidir_fft_conv(u, h_f, h_b, skip):
    L = u.shape[1]
    filt = jnp.concatenate([h_f, jnp.zeros((1, h_f.shape[1]), jnp.float32),
                            jnp.flip(h_b[1:], axis=0)], axis=0)
    U = jnp.fft.rfft(u.astype(jnp.float32), n=2 * L, axis=1)
    F = jnp.fft.rfft(filt, n=2 * L, axis=0)
    y = jnp.fft.irfft(U * F[None], n=2 * L, axis=1)[:, :L]
    return (y + u.astype(jnp.float32) * skip).astype(u.dtype)


def hyena_mixer(u_raw, conv_fn, conv_w, conv_b, w1, b1, w2, b2, w3, b3, w4, freq, skip):
    u = conv_fn(u_raw, conv_w, conv_b)
    x0, x1, v = jnp.split(u, 3, axis=-1)
    h_f, h_b = hyena_filters(u.shape[1], w1, b1, w2, b2, w3, b3, w4, freq)
    return x0 * bidir_fft_conv(x1 * v, h_f, h_b, skip)


def merge_groups(p, y_s, y_h, gw):
    ys = rmsnorm(y_s * jax.nn.silu(p[..., COL_ZS:COL_XBC]), gw[:SSD_W])
    yh = rmsnorm(y_h * jax.nn.silu(p[..., COL_ZH:COL_HY]), gw[SSD_W:])
    return jnp.concatenate([ys, yh], axis=-1)


def setup_inputs(seed: int = 0) -> dict:
    key = jax.random.key(seed)
    ks = jax.random.split(key, 27)
    nrm = lambda k, shape, s: jax.random.normal(k, shape, jnp.float32) * s
    dt0 = jnp.exp(jax.random.uniform(ks[10], (DEPTH, 2, SSD_HEADS), jnp.float32,
                                     math.log(1e-3), math.log(1e-1)))
    dt_bias = dt0 + jnp.log(-jnp.expm1(-dt0))
    a_log = jnp.log(jax.random.uniform(ks[11], (DEPTH, 2, SSD_HEADS), jnp.float32, 1.0, 16.0))
    return {
        'x': nrm(ks[0], (BATCH, SEQ, D_MODEL), 1.0),
        'c': nrm(ks[1], (BATCH, D_MODEL), 1.0),
        'ctx': nrm(ks[2], (BATCH, CTX_LEN, D_MODEL), 1.0),
        'c_ctx': nrm(ks[3], (D_MODEL,), 1.0),
        'norm_w': 1.0 + nrm(ks[4], (DEPTH, D_MODEL), 0.02),
        'ada_w': nrm(ks[5], (DEPTH, D_MODEL, 3 * D_MODEL), 0.5 * D_MODEL ** -0.5),
        'ada_b': nrm(ks[6], (DEPTH, 3 * D_MODEL), 0.02),
        'w_in': nrm(ks[7], (DEPTH, D_MODEL, IN_W), D_MODEL ** -0.5),
        'ssd_conv_w': nrm(ks[8], (DEPTH, SSD_CONV, XBC_W), SSD_CONV ** -0.5),
        'ssd_conv_b': nrm(ks[9], (DEPTH, XBC_W), 0.02),
        'dt_bias': dt_bias,
        'a_log': a_log,
        'd_skip': 1.0 + nrm(ks[12], (DEPTH, 2, SSD_HEADS), 0.02),
        'hy_conv_w': nrm(ks[13], (DEPTH, HY_CONV, 3 * HY_W), HY_CONV ** -0.5),
        'hy_conv_b': nrm(ks[14], (DEPTH, 3 * HY_W), 0.02),
        'filt_w1': nrm(ks[15], (DEPTH, HY_EMB, HY_FILT_W), HY_EMB ** -0.5),
        'filt_b1': nrm(ks[16], (DEPTH, HY_FILT_W), 0.02),
        'filt_w2': nrm(ks[17], (DEPTH, HY_FILT_W, HY_FILT_W), HY_FILT_W ** -0.5),
        'filt_b2': nrm(ks[18], (DEPTH, HY_FILT_W), 0.02),
        'filt_w3': nrm(ks[19], (DEPTH, HY_FILT_W, HY_FILT_W), HY_FILT_W ** -0.5),
        'filt_b3': nrm(ks[20], (DEPTH, HY_FILT_W), 0.02),
        'filt_w4': nrm(ks[21], (DEPTH, HY_FILT_W, 2 * HY_W), HY_FILT_W ** -0.5),
        'filt_freq': 1.0 + nrm(ks[22], (DEPTH, HY_FILT_W), 0.02),
        'hy_bias': nrm(ks[23], (DEPTH, HY_W), 0.5),
        'gnorm_w': 1.0 + nrm(ks[24], (DEPTH, MIX_W), 0.02),
        'w_out': nrm(ks[25], (DEPTH, MIX_W, D_MODEL), MIX_W ** -0.5),
        'final_norm_w': 1.0 + nrm(ks[26], (D_MODEL,), 0.02),
    }


def reference(x, c, ctx, c_ctx, norm_w, ada_w, ada_b, w_in, ssd_conv_w, ssd_conv_b,
              dt_bias, a_log, d_skip, hy_conv_w, hy_conv_b, filt_w1, filt_b1, filt_w2,
              filt_b2, filt_w3, filt_b3, filt_w4, filt_freq, hy_bias, gnorm_w, w_out,
              final_norm_w):
    bsz = x.shape[0]
    hc = ctx
    s_lat = jax.nn.silu(c)
    s_ctx = jax.nn.silu(c_ctx)
    for l in range(DEPTH):
        shift_x, scale_x, gate_x = jnp.split(s_lat @ ada_w[l] + ada_b[l], 3, axis=-1)
        shift_c, scale_c, gate_c = jnp.split(s_ctx @ ada_w[l] + ada_b[l], 3, axis=-1)
        p_x = (rmsnorm(x, norm_w[l]) * (1.0 + scale_x[:, None]) + shift_x[:, None]) @ w_in[l]
        p_c = (rmsnorm(hc, norm_w[l]) * (1.0 + scale_c) + shift_c) @ w_in[l]
        ssd_p = (ssd_conv_w[l], ssd_conv_b[l], dt_bias[l], a_log[l], d_skip[l])
        hy_p = (hy_conv_w[l], hy_conv_b[l], filt_w1[l], filt_b1[l], filt_w2[l], filt_b2[l],
                filt_w3[l], filt_b3[l], filt_w4[l], filt_freq[l], hy_bias[l])
        h0 = jnp.zeros((bsz, SSD_GROUPS, SSD_HEADS // SSD_GROUPS, SSD_HEADDIM, SSD_STATE),
                       jnp.float32)
        y_sc, hf_c, hb_c = ssd_mixer(p_c[..., COL_XBC:COL_DT], p_c[..., COL_DT:COL_ZH],
                                     seq_conv, *ssd_p, h0, h0)
        y_sx, _, _ = ssd_mixer(p_x[..., COL_XBC:COL_DT], p_x[..., COL_DT:COL_ZH],
                               grid_row_conv, *ssd_p, hf_c, hb_c)
        y_hx = hyena_mixer(p_x[..., COL_HY:], grid_row_conv, *hy_p)
        x = x + gate_x[:, None] * (merge_groups(p_x, y_sx, y_hx, gnorm_w[l]) @ w_out[l])
        if l < DEPTH - 1:
            y_hc = hyena_mixer(p_c[..., COL_HY:], seq_conv, *hy_p)
            hc = hc + gate_c * (merge_groups(p_c, y_sc, y_hc, gnorm_w[l]) @ w_out[l])
    return rmsnorm(x, final_norm_w)
```

```python
import functools
import math

import numpy as np
import jax
import jax.numpy as jnp
from jax import lax
from jax.experimental import pallas as pl
from jax.experimental.pallas import tpu as pltpu

D_MODEL = 1024
SEQ = 8192
CTX_LEN = 256
GRID_W = 64
SSD_W = 1024
HY_W = 1024
SSD_HEADDIM = 64
SSD_HEADS = SSD_W // SSD_HEADDIM
SSD_GROUPS = 2
SSD_STATE = 128
CHUNK = 128
HY_EMB = 33
HY_BANDS = (HY_EMB - 1) // 2
HY_FILT_W = 64
HY_TARGET = 1e-2
HY_MIN_DECAY = math.log(HY_TARGET) / 1.5
HY_MAX_DECAY = math.log(HY_TARGET) / 0.3
EPS = 1e-6
XBC_W = SSD_W + 2 * SSD_GROUPS * SSD_STATE
COL_ZS = 0
COL_XBC = COL_ZS + SSD_W
COL_DT = COL_XBC + XBC_W
COL_ZH = COL_DT + 2 * SSD_HEADS
COL_HY = COL_ZH + HY_W

LANES = 128
T_ALL = SEQ + CTX_LEN
TT = 256
LAT_TILES = SEQ // TT
ALL_TILES = T_ALL // TT
HY_CTX_ROWS = 8
T_HY = SEQ + HY_CTX_ROWS * LANES
HY_TILES = T_HY // TT
DT_ROWS = 128
N_CHUNKS = T_ALL // CHUNK
LAT_CHUNKS = SEQ // CHUNK
R_ZS = 0
R_XBC = R_ZS + SSD_W
R_DT = R_XBC + XBC_W
R_ZH = R_DT + DT_ROWS
R_HY = R_ZH + HY_W
R_END = R_HY + 3 * HY_W

ACT = jnp.float32
BF = jnp.bfloat16
F32 = jnp.float32
HI = lax.Precision.HIGHEST
VMEM_LIMIT = 56 * 1024 * 1024


def _cp(sem):
    return pltpu.CompilerParams(dimension_semantics=sem, vmem_limit_bytes=VMEM_LIMIT)


def _silu(v):
    return v * jax.nn.sigmoid(v)


def _to_cm_kernel(x_ref, ctx_ref, o_ref):
    t = pl.program_id(1)

    @pl.when(t < LAT_TILES)
    def _():
        o_ref[0] = x_ref[0].T

    @pl.when(t == LAT_TILES)
    def _():
        o_ref[0] = ctx_ref[0].T


def _to_channel_major(x, ctx):
    b = x.shape[0]
    return pl.pallas_call(
        _to_cm_kernel,
        grid=(b, ALL_TILES),
        in_specs=[
            pl.BlockSpec((1, TT, D_MODEL), lambda i, t: (i, jnp.minimum(t, LAT_TILES - 1), 0)),
            pl.BlockSpec((1, CTX_LEN, D_MODEL), lambda i, t: (i, 0, 0)),
        ],
        out_specs=pl.BlockSpec((1, D_MODEL, TT), lambda i, t: (i, 0, t)),
        out_shape=jax.ShapeDtypeStruct((b, D_MODEL, T_ALL), F32),
        compiler_params=_cp(("parallel", "arbitrary")),
        name="to_channel_major",
    )(x, ctx)


def _ada_kernel(s_ref, w_ref, b_ref, o_ref):
    s = _silu(s_ref[...])
    o_ref[0] = jnp.dot(s, w_ref[0], precision=HI, preferred_element_type=F32) + b_ref[0]


def _ada(s_rows, ada_w, ada_b):
    depth, d, n3 = ada_w.shape
    ct = 1024
    return pl.pallas_call(
        _ada_kernel,
        grid=(depth, n3 // ct),
        in_specs=[
            pl.BlockSpec((8, d), lambda l, j: (0, 0)),
            pl.BlockSpec((1, d, ct), lambda l, j: (l, 0, j)),
            pl.BlockSpec((1, 1, ct), lambda l, j: (l, 0, j)),
        ],
        out_specs=pl.BlockSpec((1, 8, ct), lambda l, j: (l, 0, j)),
        out_shape=jax.ShapeDtypeStruct((depth, 8, n3), F32),
        compiler_params=_cp(("arbitrary", "arbitrary")),
        name="ada_modulation",
    )(s_rows, ada_w, ada_b.reshape(depth, 1, n3))


def _inproj_kernel(x_ref, mod_ref, nw_ref, w_ref, scw_ref, hcw_ref, ml_ref, mr_ref,
                   zs_ref, xbc_ref, dt_ref, zh_ref, x0_ref, u_ref):
    t = pl.program_id(1)

    @pl.when(t < ALL_TILES)
    def _():
        x = x_ref[0]
        ms = jnp.mean(x * x, axis=0, keepdims=True)
        xn = x * lax.rsqrt(ms + EPS)
        shift = mod_ref[0, 0, :, 0:1]
        scale1 = mod_ref[0, 0, :, 1:2]
        xb = ((xn * nw_ref[...]) * scale1 + shift).astype(BF)
        ml = ml_ref[...]
        mr = mr_ref[...]

        def mm(r0, rows):
            return jnp.dot(w_ref[r0:r0 + rows, :], xb, preferred_element_type=F32)

        def conv(p, cw_ref, r0, rows):
            cw = cw_ref[r0:r0 + rows, :]
            left = pltpu.roll(p, 1, axis=1) * ml
            right = pltpu.roll(p, TT - 1, axis=1) * mr
            return cw[:, 0:1] * left + cw[:, 1:2] * p + cw[:, 2:3] * right + cw[:, 3:4]

        rc = 512
        for r in range(0, SSD_W, rc):
            zs_ref[0, r:r + rc, :] = _silu(mm(R_ZS + r, rc)).astype(zs_ref.dtype)
        for r in range(0, XBC_W, rc):
            xbc_ref[0, r:r + rc, :] = _silu(conv(mm(R_XBC + r, rc), scw_ref, r, rc)).astype(xbc_ref.dtype)
        dt_ref[0] = mm(R_DT, DT_ROWS)
        for r in range(0, HY_W, rc):
            zh_ref[0, r:r + rc, :] = _silu(mm(R_ZH + r, rc)).astype(zh_ref.dtype)
        for r in range(0, HY_W, rc):
            p0 = conv(mm(R_HY + r, rc), hcw_ref, r, rc)
            p1 = conv(mm(R_HY + HY_W + r, rc), hcw_ref, HY_W + r, rc)
            pv = conv(mm(R_HY + 2 * HY_W + r, rc), hcw_ref, 2 * HY_W + r, rc)
            x0_ref[0, r:r + rc, :] = p0.astype(x0_ref.dtype)
            u_ref[0, r:r + rc, :] = (p1 * pv).astype(u_ref.dtype)

    @pl.when(t >= ALL_TILES)
    def _():
        x0_ref[...] = jnp.zeros_like(x0_ref)
        u_ref[...] = jnp.zeros_like(u_ref)


def _inproj(xt, mod, nw, w_t, scw, hcw, ml, mr):
    b = xt.shape[0]
    last = ALL_TILES - 1
    tok = lambda i, t: (i, 0, jnp.minimum(t, last))
    const2 = lambda i, t: (0, 0)
    out_tok = lambda rows: pl.BlockSpec((1, rows, TT), tok)
    return pl.pallas_call(
        _inproj_kernel,
        grid=(b, HY_TILES),
        in_specs=[
            pl.BlockSpec((1, D_MODEL, TT), tok),
            pl.BlockSpec((1, 1, D_MODEL, 4), lambda i, t: (i, jnp.where(t >= LAT_TILES, 1, 0), 0, 0)),
            pl.BlockSpec((D_MODEL, 1), const2),
            pl.BlockSpec((R_END, D_MODEL), const2),
            pl.BlockSpec((XBC_W, 4), const2),
            pl.BlockSpec((3 * HY_W, 4), const2),
            pl.BlockSpec((1, TT), lambda i, t: (0, jnp.minimum(t, last))),
            pl.BlockSpec((1, TT), lambda i, t: (0, jnp.minimum(t, last))),
        ],
        out_specs=[
            out_tok(SSD_W), out_tok(XBC_W), out_tok(DT_ROWS), out_tok(HY_W),
            pl.BlockSpec((1, HY_W, TT), lambda i, t: (i, 0, t)),
            pl.BlockSpec((1, HY_W, TT), lambda i, t: (i, 0, t)),
        ],
        out_shape=[
            jax.ShapeDtypeStruct((b, SSD_W, T_ALL), ACT),
            jax.ShapeDtypeStruct((b, XBC_W, T_ALL), ACT),
            jax.ShapeDtypeStruct((b, DT_ROWS, T_ALL), F32),
            jax.ShapeDtypeStruct((b, HY_W, T_ALL), ACT),
            jax.ShapeDtypeStruct((b, HY_W, T_HY), ACT),
            jax.ShapeDtypeStruct((b, HY_W, T_HY), ACT),
        ],
        compiler_params=_cp(("parallel", "arbitrary")),
        name="in_projection",
    )(xt, mod, nw, w_t, scw, hcw, ml, mr)


def _expand_heads(v):
    q = v.shape[1]
    return jnp.concatenate(
        [jnp.broadcast_to(v[h:h + 1, :], (SSD_HEADDIM, q)) for h in range(SSD_HEADS)], axis=0)


def _ssd_direction(blk, cum_t, cum_c, tot_b, dt, valid, d_exp, st_ref):
    q = CHUNK
    hp = SSD_W // SSD_GROUPS
    x = blk[0:SSD_W, :].astype(F32)
    xdt = x * _expand_heads(dt)
    xdt_b = xdt.astype(BF)
    w_b = (xdt * _expand_heads(jnp.exp(tot_b - cum_t))).astype(BF)
    in_scale = _expand_heads(jnp.exp(cum_t))
    s_prev = st_ref[...]
    s_prev_b = s_prev.astype(BF)
    ys = []
    s_new = []
    hpg = SSD_HEADS // SSD_GROUPS
    for g in range(SSD_GROUPS):
        b_t = blk[SSD_W + g * SSD_STATE:SSD_W + (g + 1) * SSD_STATE, :].astype(BF)
        c_t = blk[SSD_W + (SSD_GROUPS + g) * SSD_STATE:SSD_W + (SSD_GROUPS + g + 1) * SSD_STATE, :].astype(BF)
        cb_t = lax.dot_general(b_t, c_t, (((0,), (0,)), ((), ())), preferred_element_type=F32)
        intra = []
        for h in range(g * hpg, (g + 1) * hpg):
            seg = cum_t[h:h + 1, :] - cum_c[:, h:h + 1]
            m_t = (cb_t * jnp.exp(jnp.where(valid, seg, -jnp.inf))).astype(BF)
            intra.append(jnp.dot(xdt_b[h * SSD_HEADDIM:(h + 1) * SSD_HEADDIM, :], m_t,
                                 preferred_element_type=F32))
        s_new.append(lax.dot_general(w_b[g * hp:(g + 1) * hp, :], b_t, (((1,), (1,)), ((), ())),
                                     preferred_element_type=F32))
        inter = jnp.dot(s_prev_b[g * hp:(g + 1) * hp, :], c_t, preferred_element_type=F32)
        ys.append(jnp.concatenate(intra, axis=0) + inter * in_scale[g * hp:(g + 1) * hp, :])
    y = jnp.concatenate(ys, axis=0) + d_exp * x
    st_ref[...] = _expand_heads(jnp.exp(tot_b)) * s_prev + jnp.concatenate(s_new, axis=0)
    return y


def _ssd_kernel(xf_ref, xb_ref, dtf_ref, dtb_ref, par_ref, dexp_ref, yf_ref, yb_ref, sf_ref, sb_ref):
    s = pl.program_id(1)

    @pl.when(s == 0)
    def _():
        sf_ref[...] = jnp.zeros_like(sf_ref)
        sb_ref[...] = jnp.zeros_like(sb_ref)

    q = CHUNK
    h = SSD_HEADS
    row = lax.broadcasted_iota(jnp.int32, (q, q), 0)
    col = lax.broadcasted_iota(jnp.int32, (q, q), 1)
    le = row <= col
    ge = row >= col
    ones = jnp.ones((q, q), F32)
    bias = par_ref[:, 0:1]
    a_neg = -jnp.exp(par_ref[:, 1:2])
    dt_f = jax.nn.softplus(dtf_ref[0, 0:h, :] + bias[0:h])
    dt_b = jax.nn.softplus(dtb_ref[0, h:2 * h, :] + bias[h:2 * h])
    a_f = dt_f * a_neg[0:h]
    a_b = dt_b * a_neg[h:2 * h]
    hdot = functools.partial(jnp.dot, precision=HI, preferred_element_type=F32)
    cum_f = hdot(a_f, le.astype(F32))
    cum_b = hdot(a_b, ge.astype(F32))
    tot_f = hdot(a_f, ones)
    tot_b = hdot(a_b, ones)
    cum_cols = jnp.concatenate([cum_f, cum_b, jnp.zeros((q - 2 * h, q), F32)], axis=0).T
    yf_ref[0] = _ssd_direction(xf_ref[0], cum_f, cum_cols[:, 0:h], tot_f, dt_f, le,
                               dexp_ref[0], sf_ref).astype(yf_ref.dtype)
    yb_ref[0] = _ssd_direction(xb_ref[0], cum_b, cum_cols[:, h:2 * h], tot_b, dt_b, ge,
                               dexp_ref[1], sb_ref).astype(yb_ref.dtype)


def _ssd(xbc_t, dt_t, par, dexp):
    b = xbc_t.shape[0]
    fwd = lambda i, s: (i, 0, (s + LAT_CHUNKS) % N_CHUNKS)
    bwd = lambda i, s: (i, 0, N_CHUNKS - 1 - s)
    return pl.pallas_call(
        _ssd_kernel,
        grid=(b, N_CHUNKS),
        in_specs=[
            pl.BlockSpec((1, XBC_W, CHUNK), fwd),
            pl.BlockSpec((1, XBC_W, CHUNK), bwd),
            pl.BlockSpec((1, DT_ROWS, CHUNK), fwd),
            pl.BlockSpec((1, DT_ROWS, CHUNK), bwd),
            pl.BlockSpec((2 * SSD_HEADS, 2), lambda i, s: (0, 0)),
            pl.BlockSpec((2, SSD_W, 1), lambda i, s: (0, 0, 0)),
        ],
        out_specs=[pl.BlockSpec((1, SSD_W, CHUNK), fwd), pl.BlockSpec((1, SSD_W, CHUNK), bwd)],
        out_shape=[jax.ShapeDtypeStruct((b, SSD_W, T_ALL), ACT)] * 2,
        scratch_shapes=[pltpu.VMEM((SSD_W, SSD_STATE), F32)] * 2,
        compiler_params=_cp(("parallel", "arbitrary")),
        name="ssd_scan",
    )(xbc_t, xbc_t, dt_t, dt_t, par, dexp)


def _filt_hidden_kernel(f_ref, w1_ref, w2_ref, w3_ref, bf_ref, o_ref):
    hdot = functools.partial(jnp.dot, precision=HI, preferred_element_type=F32)
    fr = bf_ref[:, 3:4]
    hcur = jnp.sin(fr * (hdot(w1_ref[...], f_ref[...]) + bf_ref[:, 0:1]))
    hcur = jnp.sin(fr * (hdot(w2_ref[...], hcur) + bf_ref[:, 1:2]))
    o_ref[...] = jnp.sin(fr * (hdot(w3_ref[...], hcur) + bf_ref[:, 2:3]))


def _filt_kernel(h_ref, w4_ref, pos_ref, dl_ref, o_ref):
    hdot = functools.partial(jnp.dot, precision=HI, preferred_element_type=F32)
    hid = h_ref[...]
    f_fwd = hdot(w4_ref[0], hid)
    f_bwd = hdot(w4_ref[1], hid)
    window = jnp.exp(-dl_ref[...] * pos_ref[0:1, :])
    o_ref[...] = (f_fwd * pos_ref[1:2, :] + f_bwd * pos_ref[2:3, :]) * window


def _hyena_filter(feats, pos, w1t, w2t, w3t, bfr, w4t, deltas):
    n = feats.shape[1]
    nt = min(n, 2048)
    full = lambda shape: pl.BlockSpec(shape, lambda j: tuple(0 for _ in shape))
    hid = pl.pallas_call(
        _filt_hidden_kernel,
        grid=(n // nt,),
        in_specs=[pl.BlockSpec((HY_FILT_W, nt), lambda j: (0, j)),
                  full((HY_FILT_W, HY_FILT_W)), full((HY_FILT_W, HY_FILT_W)),
                  full((HY_FILT_W, HY_FILT_W)), full((HY_FILT_W, 4))],
        out_specs=pl.BlockSpec((HY_FILT_W, nt), lambda j: (0, j)),
        out_shape=jax.ShapeDtypeStruct((HY_FILT_W, n), F32),
        compiler_params=_cp(("arbitrary",)),
        name="hyena_filter_hidden",
    )(feats, w1t, w2t, w3t, bfr)
    ct = 256
    return pl.pallas_call(
        _filt_kernel,
        grid=(HY_W // ct, n // nt),
        in_specs=[pl.BlockSpec((HY_FILT_W, nt), lambda c, j: (0, j)),
                  pl.BlockSpec((2, ct, HY_FILT_W), lambda c, j: (0, c, 0)),
                  pl.BlockSpec((8, nt), lambda c, j: (0, j)),
                  pl.BlockSpec((ct, 1), lambda c, j: (c, 0))],
        out_specs=pl.BlockSpec((ct, nt), lambda c, j: (c, j)),
        out_shape=jax.ShapeDtypeStruct((HY_W, n), F32),
        compiler_params=_cp(("parallel", "arbitrary")),
        name="hyena_filter",
    )(hid, w4t, pos, deltas)


def _hyconv_kernel(u_ref, x0_ref, f_ref, skip_ref, g1_ref, g1f_ref, g1i_ref, tw_ref, w3_ref, w3c_ref,
                   o_ref, l3_ref, l3f_ref, bp_ref, *, n1, r, ct):
    twr = tw_ref[0]
    twi = tw_ref[1]

    def fwd_body(c, carry):
        rhs = jnp.concatenate([u_ref[0, c], u_ref[1, c]], axis=0).astype(BF)
        bcol = jnp.dot(g1_ref[...], rhs, preferred_element_type=F32)
        br, bi = bcol[:n1], bcol[n1:]
        rows = pl.ds(pl.multiple_of(c * n1, n1), n1)
        l3_ref[rows, 0:LANES] = (br * twr - bi * twi).astype(BF)
        l3_ref[rows, LANES:2 * LANES] = (br * twi + bi * twr).astype(BF)
        fcol = jnp.dot(g1f_ref[...], f_ref[c].astype(BF), preferred_element_type=F32)
        fr, fi = fcol[:n1], fcol[n1:]
        l3f_ref[rows, 0:LANES] = (fr * twr - fi * twi).astype(BF)
        l3f_ref[rows, LANES:2 * LANES] = (fr * twi + fi * twr).astype(BF)
        return carry

    lax.fori_loop(0, ct, fwd_body, 0)

    rows_per = 256
    def spec_body(k, carry):
        rows = pl.ds(pl.multiple_of(k * rows_per, rows_per), rows_per)
        xs = jnp.dot(l3_ref[rows, :], w3_ref[...], preferred_element_type=F32)
        hs = jnp.dot(l3f_ref[rows, :], w3_ref[...], preferred_element_type=F32)
        xr, xi = xs[:, :LANES], xs[:, LANES:]
        hr, hi = hs[:, :LANES], hs[:, LANES:]
        ys = jnp.concatenate([xr * hr - xi * hi, xr * hi + xi * hr], axis=1).astype(BF)
        bp_ref[rows, :] = jnp.dot(ys, w3c_ref[...], preferred_element_type=F32)
        return carry

    lax.fori_loop(0, ct * n1 // rows_per, spec_body, 0)

    def inv_body(c, carry):
        rows = pl.ds(pl.multiple_of(c * n1, n1), n1)
        bp = bp_ref[rows, :]
        br, bi = bp[:, :LANES], bp[:, LANES:]
        rhs = jnp.concatenate([br * twr + bi * twi, bi * twr - br * twi], axis=0).astype(BF)
        out = jnp.dot(g1i_ref[...], rhs, preferred_element_type=F32)
        sk = skip_ref[c]
        for bb in range(2):
            ub = u_ref[bb, c].astype(F32)
            o_ref[bb, c] = (x0_ref[bb, c].astype(F32) * (out[bb * r:(bb + 1) * r] + sk * ub)).astype(o_ref.dtype)
        return carry

    lax.fori_loop(0, ct, inv_body, 0)


@functools.lru_cache(maxsize=None)
def _dft_constants(n1, r):
    n = n1 * LANES
    k1 = np.arange(n1)
    f1 = np.exp(-2j * np.pi * np.outer(k1, k1) / n1)
    f2 = np.exp(-2j * np.pi * np.outer(np.arange(LANES), np.arange(LANES)) / LANES)
    tw = np.exp(-2j * np.pi * np.outer(k1, np.arange(LANES)) / n)
    g1 = np.block([[f1.real[:, :r], -f1.imag[:, :r]], [f1.imag[:, :r], f1.real[:, :r]]])
    g1f = np.concatenate([f1.real, f1.imag], axis=0)
    g1i = np.block([[f1.real[:r], f1.imag[:r]], [-f1.imag[:r], f1.real[:r]]]) / n
    w3 = np.block([[f2.real, f2.imag], [-f2.imag, f2.real]])
    w3c = np.block([[f2.real, -f2.imag], [f2.imag, f2.real]])
    tws = np.stack([tw.real, tw.imag])
    return tuple(np.asarray(a, np.float32) for a in (g1, g1f, g1i, tws, w3, w3c))


def _hyconv(u_t, x0_t, filt, skip_b, prev, *, n1, r, row_block):
    b, c_all = u_t.shape[0], u_t.shape[1]
    ct = 16 if n1 * 16 >= 256 else 256 // n1
    g1, g1f, g1i, tws, w3, w3c = (jnp.asarray(a) for a in _dft_constants(n1, r))
    g1, g1f, g1i, w3, w3c = (a.astype(BF) for a in (g1, g1f, g1i, w3, w3c))
    sig = pl.BlockSpec((b, ct, r, LANES), lambda i: (0, i, row_block, 0))
    full = lambda a: pl.BlockSpec(a.shape, lambda i: tuple(0 for _ in a.shape))
    args = [u_t, x0_t, filt, skip_b, g1, g1f, g1i, tws, w3, w3c]
    in_specs = [sig, sig, pl.BlockSpec((ct, n1, LANES), lambda i: (i, 0, 0)),
                pl.BlockSpec((ct, 1, LANES), lambda i: (i, 0, 0)),
                full(g1), full(g1f), full(g1i), full(tws), full(w3), full(w3c)]
    aliases = {}
    if prev is not None:
        args.append(prev)
        in_specs.append(pl.BlockSpec(memory_space=pl.ANY))
        aliases = {len(args) - 1: 0}
    kern = functools.partial(_hyconv_kernel, n1=n1, r=r, ct=ct)
    if prev is not None:
        body = kern
        kern = lambda *refs: body(*refs[:10], *refs[11:])
    return pl.pallas_call(
        kern,
        grid=(c_all // ct,),
        in_specs=in_specs,
        out_specs=sig,
        out_shape=jax.ShapeDtypeStruct(u_t.shape, ACT),
        scratch_shapes=[pltpu.VMEM((ct * n1, 2 * LANES), BF), pltpu.VMEM((ct * n1, 2 * LANES), BF),
                        pltpu.VMEM((ct * n1, 2 * LANES), F32)],
        input_output_aliases=aliases,
        compiler_params=_cp(("arbitrary",)),
        name="hyena_conv_n%d" % n1,
    )(*args)


def _merge_kernel(yf_ref, yb_ref, zs_ref, yh_ref, zh_ref, x_ref, mod_ref, gw_ref, w_ref, fw_ref,
                  o_ref, *, final):
    def gated_norm(y, z, gw):
        g = y * z
        ms = jnp.mean(g * g, axis=0, keepdims=True)
        return ((g * lax.rsqrt(ms + EPS)) * gw).astype(BF)

    ms_ = gated_norm(yf_ref[0].astype(F32) + yb_ref[0].astype(F32), zs_ref[0].astype(F32), gw_ref[0:SSD_W, :])
    mh_ = gated_norm(yh_ref[0].astype(F32), zh_ref[0].astype(F32), gw_ref[SSD_W:, :])
    merged = jnp.concatenate([ms_, mh_], axis=0)
    out = jnp.dot(w_ref[...], merged, preferred_element_type=F32)
    xn = x_ref[0] + mod_ref[0, 0, :, 2:3] * out
    if final:
        ms = jnp.mean(xn * xn, axis=0, keepdims=True)
        o_ref[0] = ((xn * lax.rsqrt(ms + EPS)) * fw_ref[...]).T
    else:
        o_ref[0] = xn


def _merge(yf, yb, zs, yh, zh, xt, mod, gw, w_out_t, fw, *, final):
    b = xt.shape[0]
    tiles = LAT_TILES if final else ALL_TILES
    tok = lambda i, t: (i, 0, t)
    const2 = lambda i, t: (0, 0)
    act = lambda rows: pl.BlockSpec((1, rows, TT), tok)
    if final:
        out_spec = pl.BlockSpec((1, TT, D_MODEL), lambda i, t: (i, t, 0))
        out_shape = jax.ShapeDtypeStruct((b, SEQ, D_MODEL), F32)
    else:
        out_spec = act(D_MODEL)
        out_shape = jax.ShapeDtypeStruct((b, D_MODEL, T_ALL), F32)
    return pl.pallas_call(
        functools.partial(_merge_kernel, final=final),
        grid=(b, tiles),
        in_specs=[act(SSD_W), act(SSD_W), act(SSD_W), act(HY_W), act(HY_W), act(D_MODEL),
                  pl.BlockSpec((1, 1, D_MODEL, 4), lambda i, t: (i, jnp.where(t >= LAT_TILES, 1, 0), 0, 0)),
                  pl.BlockSpec((SSD_W + HY_W, 1), const2),
                  pl.BlockSpec((D_MODEL, SSD_W + HY_W), const2),
                  pl.BlockSpec((D_MODEL, 1), const2)],
        out_specs=out_spec,
        out_shape=out_shape,
        compiler_params=_cp(("parallel", "arbitrary")),
        name="merge_out_projection",
    )(yf, yb, zs, yh, zh, xt, mod, gw, w_out_t, fw)


@functools.lru_cache(maxsize=None)
def _conv_masks():
    t = np.arange(T_ALL)
    pos = np.where(t < SEQ, t % GRID_W, t - SEQ)
    period = np.where(t < SEQ, GRID_W, CTX_LEN)
    ml = (pos != 0).astype(np.float32)[None, :]
    mr = (pos != period - 1).astype(np.float32)[None, :]
    return ml, mr


@functools.lru_cache(maxsize=None)
def _filter_positions(n, length):
    idx = np.arange(n)
    fwd = idx < length
    bwd = idx > n - length
    d = np.where(fwd, idx, np.where(bwd, n - idx, 0)).astype(np.float64)
    t = (np.linspace(0.0, 1.0, length, dtype=np.float32).astype(np.float64))[d.astype(np.int64)]
    w = (2.0 * math.pi / length) * d
    f = np.linspace(1e-4, HY_BANDS - 1, HY_BANDS, dtype=np.float32).astype(np.float64)[:, None]
    feats = np.zeros((HY_FILT_W, n), np.float32)
    feats[0] = t
    feats[1:1 + HY_BANDS] = np.cos(f * w[None, :])
    feats[1 + HY_BANDS:HY_EMB] = -np.sin(f * w[None, :])
    pos = np.zeros((8, n), np.float32)
    pos[0] = t
    pos[1] = fwd
    pos[2] = bwd
    return feats, pos


def kernel(x, c, ctx, c_ctx, norm_w, ada_w, ada_b, w_in, ssd_conv_w, ssd_conv_b, dt_bias, a_log, d_skip,
           hy_conv_w, hy_conv_b, filt_w1, filt_b1, filt_w2, filt_b2, filt_w3, filt_b3, filt_w4, filt_freq,
           hy_bias, gnorm_w, w_out, final_norm_w):
    bsz = x.shape[0]
    depth = norm_w.shape[0]
    assert x.shape == (bsz, SEQ, D_MODEL) and ctx.shape == (bsz, CTX_LEN, D_MODEL) and bsz == 2

    s_rows = jnp.zeros((8, D_MODEL), F32).at[:bsz].set(c).at[bsz].set(c_ctx)
    mods = _ada(s_rows, ada_w, ada_b)
    mods = mods[:, :bsz + 1].reshape(depth, bsz + 1, 3, D_MODEL)
    lat = mods[:, :bsz]
    cx = jnp.broadcast_to(mods[:, bsz:bsz + 1], lat.shape)
    mod = jnp.stack([lat, cx], axis=2)
    mod = jnp.stack([mod[:, :, :, 0], 1.0 + mod[:, :, :, 1], mod[:, :, :, 2], jnp.zeros_like(mod[:, :, :, 0])],
                    axis=-1)

    ml, mr = (jnp.asarray(a) for a in _conv_masks())
    deltas = jnp.abs(jnp.linspace(HY_MIN_DECAY, HY_MAX_DECAY, HY_W, dtype=F32))[:, None]
    xt = _to_channel_major(x, ctx)

    out = None
    for l in range(depth):
        last = l == depth - 1
        wl = w_in[l]
        w_t = jnp.concatenate([
            wl[:, COL_ZS:COL_XBC].T, wl[:, COL_XBC:COL_DT].T,
            jnp.pad(wl[:, COL_DT:COL_ZH].T, ((0, DT_ROWS - 2 * SSD_HEADS), (0, 0))),
            wl[:, COL_ZH:COL_HY].T, wl[:, COL_HY:].T], axis=0).astype(BF)
        scw = jnp.concatenate([ssd_conv_w[l].T, ssd_conv_b[l][:, None]], axis=1)
        hcw = jnp.concatenate([hy_conv_w[l].T, hy_conv_b[l][:, None]], axis=1)
        zs, xbc, dtt, zh, x0, u = _inproj(xt, mod[l], norm_w[l][:, None], w_t, scw, hcw, ml, mr)

        par = jnp.stack([dt_bias[l].reshape(-1), a_log[l].reshape(-1)], axis=1)
        dexp = jnp.repeat(d_skip[l], SSD_HEADDIM, axis=1)[:, :, None]
        yf, yb = _ssd(xbc, dtt, par, dexp)

        w4t = filt_w4[l].T.reshape(2, HY_W, HY_FILT_W)
        bfr = jnp.stack([filt_b1[l], filt_b2[l], filt_b3[l], filt_freq[l]], axis=1)
        w1t = jnp.pad(filt_w1[l].T, ((0, 0), (0, HY_FILT_W - HY_EMB)))
        skip_b = jnp.broadcast_to(hy_bias[l][:, None, None], (HY_W, 1, LANES))
        rows_hy = T_HY // LANES
        u4 = u.reshape(bsz, HY_W, rows_hy, LANES)
        x04 = x0.reshape(bsz, HY_W, rows_hy, LANES)

        def long_conv(n1, r, row_block, length, prev):
            feats, pos = (jnp.asarray(a) for a in _filter_positions(n1 * LANES, length))
            filt = _hyena_filter(feats, pos, w1t, filt_w2[l].T, filt_w3[l].T, bfr, w4t, deltas)
            return _hyconv(u4, x04, filt.reshape(HY_W, n1, LANES), skip_b, prev,
                           n1=n1, r=r, row_block=row_block)

        yh = long_conv(2 * SEQ // LANES, SEQ // LANES, 0, SEQ, None)
        if not last:
            yh = long_conv(16, HY_CTX_ROWS, SEQ // LANES // HY_CTX_ROWS, CTX_LEN, yh)
        yh = yh.reshape(bsz, HY_W, T_HY)

        res = _merge(yf, yb, zs, yh, zh, xt, mod[l], gnorm_w[l][:, None], w_out[l].T.astype(BF),
                     final_norm_w[:, None], final=last)
        if last:
            out = res
        else:
            xt = res
    return out
```

```python
import functools
import math

import numpy as np
import jax
import jax.numpy as jnp
from jax import lax
from jax.experimental import pallas as pl
from jax.experimental.pallas import tpu as pltpu

D_MODEL = 1024
SEQ = 8192
CTX_LEN = 256
GRID_W = 64
SSD_W = 1024
HY_W = 1024
SSD_HEADDIM = 64
SSD_HEADS = SSD_W // SSD_HEADDIM
SSD_GROUPS = 2
SSD_STATE = 128
CHUNK = 128
HY_EMB = 33
HY_BANDS = (HY_EMB - 1) // 2
HY_FILT_W = 64
HY_TARGET = 1e-2
HY_MIN_DECAY = math.log(HY_TARGET) / 1.5
HY_MAX_DECAY = math.log(HY_TARGET) / 0.3
EPS = 1e-6
XBC_W = SSD_W + 2 * SSD_GROUPS * SSD_STATE
COL_ZS = 0
COL_XBC = COL_ZS + SSD_W
COL_DT = COL_XBC + XBC_W
COL_ZH = COL_DT + 2 * SSD_HEADS
COL_HY = COL_ZH + HY_W

LANES = 128
T_ALL = SEQ + CTX_LEN
TT = 256
LAT_TILES = SEQ // TT
ALL_TILES = T_ALL // TT
HY_CTX_ROWS = 8
T_HY = SEQ + HY_CTX_ROWS * LANES
HY_TILES = T_HY // TT
DT_ROWS = 128
N_CHUNKS = T_ALL // CHUNK
LAT_CHUNKS = SEQ // CHUNK
R_ZS = 0
R_XBC = R_ZS + SSD_W
R_DT = R_XBC + XBC_W
R_ZH = R_DT + DT_ROWS
R_HY = R_ZH + HY_W
R_END = R_HY + 3 * HY_W
CONV_ROWS = XBC_W + 3 * HY_W

ACT = jnp.bfloat16
BF = jnp.bfloat16
F32 = jnp.float32
HI = lax.Precision.HIGHEST
VMEM_LIMIT = 56 * 1024 * 1024


def _cp(sem):
    return pltpu.CompilerParams(dimension_semantics=sem, vmem_limit_bytes=VMEM_LIMIT)


def _silu(v):
    return v * jax.nn.sigmoid(v)


def _to_cm_kernel(x_ref, ctx_ref, o_ref):
    t = pl.program_id(1)

    @pl.when(t < LAT_TILES)
    def _():
        o_ref[0] = x_ref[0].T

    @pl.when(t == LAT_TILES)
    def _():
        o_ref[0] = ctx_ref[0].T


def _to_channel_major(x, ctx):
    b = x.shape[0]
    return pl.pallas_call(
        _to_cm_kernel,
        grid=(b, ALL_TILES),
        in_specs=[
            pl.BlockSpec((1, TT, D_MODEL), lambda i, t: (i, jnp.minimum(t, LAT_TILES - 1), 0)),
            pl.BlockSpec((1, CTX_LEN, D_MODEL), lambda i, t: (i, 0, 0)),
        ],
        out_specs=pl.BlockSpec((1, D_MODEL, TT), lambda i, t: (i, 0, t)),
        out_shape=jax.ShapeDtypeStruct((b, D_MODEL, T_ALL), F32),
        compiler_params=_cp(("parallel", "arbitrary")),
        name="to_channel_major",
    )(x, ctx)


def _ada_kernel(s_ref, w_ref, b_ref, o_ref):
    s = _silu(s_ref[...])
    o_ref[0] = jnp.dot(s, w_ref[0], precision=HI, preferred_element_type=F32) + b_ref[0]


def _ada(s_rows, ada_w, ada_b):
    depth, d, n3 = ada_w.shape
    ct = 1024
    return pl.pallas_call(
        _ada_kernel,
        grid=(depth, n3 // ct),
        in_specs=[
            pl.BlockSpec((8, d), lambda l, j: (0, 0)),
            pl.BlockSpec((1, d, ct), lambda l, j: (l, 0, j)),
            pl.BlockSpec((1, 1, ct), lambda l, j: (l, 0, j)),
        ],
        out_specs=pl.BlockSpec((1, 8, ct), lambda l, j: (l, 0, j)),
        out_shape=jax.ShapeDtypeStruct((depth, 8, n3), F32),
        compiler_params=_cp(("arbitrary", "arbitrary")),
        name="ada_modulation",
    )(s_rows, ada_w, ada_b.reshape(depth, 1, n3))


def _inproj_kernel(x_ref, mod_ref, nw_ref, w_ref, cw_ref, ml_ref, mr_ref,
                   zs_ref, xbc_ref, dt_ref, zh_ref, x0_ref, u_ref, cwb_ref):
    t = pl.program_id(1)
    rc = 256

    @pl.when(t == 0)
    def _():
        for r in range(0, CONV_ROWS, 512):
            for k in range(4):
                cwb_ref[k, r:r + 512, :] = jnp.broadcast_to(cw_ref[r:r + 512, k:k + 1], (512, LANES))

    @pl.when(t < ALL_TILES)
    def _():
        x = x_ref[0]
        ms = jnp.mean(x * x, axis=0, keepdims=True)
        xn = x * lax.rsqrt(ms + EPS)
        shift = mod_ref[0, 0, :, 0:1]
        scale1 = mod_ref[0, 0, :, 1:2]
        xb = ((xn * nw_ref[...]) * scale1 + shift).astype(BF)
        ml = ml_ref[...]
        mr = mr_ref[...]

        def mm(r0, rows):
            return jnp.dot(w_ref[r0:r0 + rows, :], xb, preferred_element_type=F32)

        def conv(p, r0, rows):
            left = pltpu.roll(p, 1, axis=1) * ml
            right = pltpu.roll(p, TT - 1, axis=1) * mr
            rs = slice(r0, r0 + rows)
            halves = []
            for hl in range(0, TT, LANES):
                ls = slice(hl, hl + LANES)
                halves.append(cwb_ref[0, rs, :] * left[:, ls] + cwb_ref[1, rs, :] * p[:, ls]
                              + cwb_ref[2, rs, :] * right[:, ls] + cwb_ref[3, rs, :])
            return jnp.concatenate(halves, axis=1)

        for r in range(0, SSD_W, rc):
            zs_ref[0, r:r + rc, :] = _silu(mm(R_ZS + r, rc)).astype(zs_ref.dtype)
        for r in range(0, XBC_W, rc):
            xbc_ref[0, r:r + rc, :] = _silu(conv(mm(R_XBC + r, rc), r, rc)).astype(xbc_ref.dtype)
        dt_ref[0] = mm(R_DT, DT_ROWS)
        for r in range(0, HY_W, rc):
            zh_ref[0, r:r + rc, :] = _silu(mm(R_ZH + r, rc)).astype(zh_ref.dtype)
        for r in range(0, HY_W, rc):
            p0 = conv(mm(R_HY + r, rc), XBC_W + r, rc)
            p1 = conv(mm(R_HY + HY_W + r, rc), XBC_W + HY_W + r, rc)
            pv = conv(mm(R_HY + 2 * HY_W + r, rc), XBC_W + 2 * HY_W + r, rc)
            x0_ref[0, r:r + rc, :] = p0.astype(x0_ref.dtype)
            u_ref[0, r:r + rc, :] = (p1 * pv).astype(u_ref.dtype)

    @pl.when(t >= ALL_TILES)
    def _():
        x0_ref[...] = jnp.zeros_like(x0_ref)
        u_ref[...] = jnp.zeros_like(u_ref)


def _inproj(xt, mod, nw, w_t, cw, ml, mr):
    b = xt.shape[0]
    last = ALL_TILES - 1
    tok = lambda i, t: (i, 0, jnp.minimum(t, last))
    const2 = lambda i, t: (0, 0)
    out_tok = lambda rows: pl.BlockSpec((1, rows, TT), tok)
    return pl.pallas_call(
        _inproj_kernel,
        grid=(b, HY_TILES),
        in_specs=[
            pl.BlockSpec((1, D_MODEL, TT), tok),
            pl.BlockSpec((1, 1, D_MODEL, 4), lambda i, t: (i, jnp.where(t >= LAT_TILES, 1, 0), 0, 0)),
            pl.BlockSpec((D_MODEL, 1), const2),
            pl.BlockSpec((R_END, D_MODEL), const2, pipeline_mode=pl.Buffered(1)),
            pl.BlockSpec((CONV_ROWS, 4), const2),
            pl.BlockSpec((1, TT), lambda i, t: (0, jnp.minimum(t, last))),
            pl.BlockSpec((1, TT), lambda i, t: (0, jnp.minimum(t, last))),
        ],
        out_specs=[
            out_tok(SSD_W), out_tok(XBC_W), out_tok(DT_ROWS), out_tok(HY_W),
            pl.BlockSpec((1, HY_W, TT), lambda i, t: (i, 0, t)),
            pl.BlockSpec((1, HY_W, TT), lambda i, t: (i, 0, t)),
        ],
        out_shape=[
            jax.ShapeDtypeStruct((b, SSD_W, T_ALL), ACT),
            jax.ShapeDtypeStruct((b, XBC_W, T_ALL), ACT),
            jax.ShapeDtypeStruct((b, DT_ROWS, T_ALL), F32),
            jax.ShapeDtypeStruct((b, HY_W, T_ALL), ACT),
            jax.ShapeDtypeStruct((b, HY_W, T_HY), ACT),
            jax.ShapeDtypeStruct((b, HY_W, T_HY), ACT),
        ],
        scratch_shapes=[pltpu.VMEM((4, CONV_ROWS, LANES), F32)],
        compiler_params=_cp(("parallel", "arbitrary")),
        name="in_projection",
    )(xt, mod, nw, w_t, cw, ml, mr)


def _expand_heads(v):
    q = v.shape[1]
    return jnp.concatenate(
        [jnp.broadcast_to(v[h:h + 1, :], (SSD_HEADDIM, q)) for h in range(SSD_HEADS)], axis=0)


def _ssd_direction(blk, cum_t, cum_c, tot_b, dt, valid, d_exp, st_ref):
    q = CHUNK
    hp = SSD_W // SSD_GROUPS
    x = blk[0:SSD_W, :].astype(F32)
    xdt = x * _expand_heads(dt)
    xdt_b = xdt.astype(BF)
    w_b = (xdt * _expand_heads(jnp.exp(tot_b - cum_t))).astype(BF)
    in_scale = _expand_heads(jnp.exp(cum_t))
    s_prev = st_ref[...]
    s_prev_b = s_prev.astype(BF)
    ys = []
    s_new = []
    hpg = SSD_HEADS // SSD_GROUPS
    for g in range(SSD_GROUPS):
        b_t = blk[SSD_W + g * SSD_STATE:SSD_W + (g + 1) * SSD_STATE, :].astype(BF)
        c_t = blk[SSD_W + (SSD_GROUPS + g) * SSD_STATE:SSD_W + (SSD_GROUPS + g + 1) * SSD_STATE, :].astype(BF)
        cb_t = lax.dot_general(b_t, c_t, (((0,), (0,)), ((), ())), preferred_element_type=F32)
        intra = []
        for h in range(g * hpg, (g + 1) * hpg):
            seg = cum_t[h:h + 1, :] - cum_c[:, h:h + 1]
            m_t = (cb_t * jnp.exp(jnp.where(valid, seg, -jnp.inf))).astype(BF)
            intra.append(jnp.dot(xdt_b[h * SSD_HEADDIM:(h + 1) * SSD_HEADDIM, :], m_t,
                                 preferred_element_type=F32))
        s_new.append(lax.dot_general(w_b[g * hp:(g + 1) * hp, :], b_t, (((1,), (1,)), ((), ())),
                                     preferred_element_type=F32))
        inter = jnp.dot(s_prev_b[g * hp:(g + 1) * hp, :], c_t, preferred_element_type=F32)
        ys.append(jnp.concatenate(intra, axis=0) + inter * in_scale[g * hp:(g + 1) * hp, :])
    y = jnp.concatenate(ys, axis=0) + d_exp * x
    st_ref[...] = _expand_heads(jnp.exp(tot_b)) * s_prev + jnp.concatenate(s_new, axis=0)
    return y


def _ssd_kernel(xf_ref, xb_ref, dtf_ref, dtb_ref, par_ref, dexp_ref, yf_ref, yb_ref, sf_ref, sb_ref):
    s = pl.program_id(1)

    @pl.when(s == 0)
    def _():
        sf_ref[...] = jnp.zeros_like(sf_ref)
        sb_ref[...] = jnp.zeros_like(sb_ref)

    q = CHUNK
    h = SSD_HEADS
    row = lax.broadcasted_iota(jnp.int32, (q, q), 0)
    col = lax.broadcasted_iota(jnp.int32, (q, q), 1)
    le = row <= col
    ge = row >= col
    ones = jnp.ones((q, q), F32)
    bias = par_ref[:, 0:1]
    a_neg = -jnp.exp(par_ref[:, 1:2])
    dt_f = jax.nn.softplus(dtf_ref[0, 0:h, :] + bias[0:h])
    dt_b = jax.nn.softplus(dtb_ref[0, h:2 * h, :] + bias[h:2 * h])
    a_f = dt_f * a_neg[0:h]
    a_b = dt_b * a_neg[h:2 * h]
    hdot = functools.partial(jnp.dot, precision=HI, preferred_element_type=F32)
    cum_f = hdot(a_f, le.astype(F32))
    cum_b = hdot(a_b, ge.astype(F32))
    tot_f = hdot(a_f, ones)
    tot_b = hdot(a_b, ones)
    cum_cols = jnp.concatenate([cum_f, cum_b, jnp.zeros((q - 2 * h, q), F32)], axis=0).T
    yf_ref[0] = _ssd_direction(xf_ref[0], cum_f, cum_cols[:, 0:h], tot_f, dt_f, le,
                               dexp_ref[0], sf_ref).astype(yf_ref.dtype)
    yb_ref[0] = _ssd_direction(xb_ref[0], cum_b, cum_cols[:, h:2 * h], tot_b, dt_b, ge,
                               dexp_ref[1], sb_ref).astype(yb_ref.dtype)


def _ssd(xbc_t, dt_t, par, dexp):
    b = xbc_t.shape[0]
    fwd = lambda i, s: (i, 0, (s + LAT_CHUNKS) % N_CHUNKS)
    bwd = lambda i, s: (i, 0, N_CHUNKS - 1 - s)
    return pl.pallas_call(
        _ssd_kernel,
        grid=(b, N_CHUNKS),
        in_specs=[
            pl.BlockSpec((1, XBC_W, CHUNK), fwd),
            pl.BlockSpec((1, XBC_W, CHUNK), bwd),
            pl.BlockSpec((1, DT_ROWS, CHUNK), fwd),
            pl.BlockSpec((1, DT_ROWS, CHUNK), bwd),
            pl.BlockSpec((2 * SSD_HEADS, 2), lambda i, s: (0, 0)),
            pl.BlockSpec((2, SSD_W, CHUNK), lambda i, s: (0, 0, 0)),
        ],
        out_specs=[pl.BlockSpec((1, SSD_W, CHUNK), fwd), pl.BlockSpec((1, SSD_W, CHUNK), bwd)],
        out_shape=[jax.ShapeDtypeStruct((b, SSD_W, T_ALL), ACT)] * 2,
        scratch_shapes=[pltpu.VMEM((SSD_W, SSD_STATE), F32)] * 2,
        compiler_params=_cp(("parallel", "arbitrary")),
        name="ssd_scan",
    )(xbc_t, xbc_t, dt_t, dt_t, par, dexp)


def _filt_hidden_kernel(f_ref, w1_ref, w2_ref, w3_ref, bf_ref, o_ref):
    hdot = functools.partial(jnp.dot, precision=HI, preferred_element_type=F32)
    fr = bf_ref[:, 3:4]
    hcur = jnp.sin(fr * (hdot(w1_ref[...], f_ref[...]) + bf_ref[:, 0:1]))
    hcur = jnp.sin(fr * (hdot(w2_ref[...], hcur) + bf_ref[:, 1:2]))
    o_ref[...] = jnp.sin(fr * (hdot(w3_ref[...], hcur) + bf_ref[:, 2:3]))


def _filt_kernel(h_ref, w4_ref, pos_ref, dl_ref, o_ref):
    hdot = functools.partial(jnp.dot, precision=HI, preferred_element_type=F32)
    hid = h_ref[...]
    f_fwd = hdot(w4_ref[0], hid)
    f_bwd = hdot(w4_ref[1], hid)
    window = jnp.exp(-dl_ref[...] * pos_ref[0:1, :])
    filt = (f_fwd * pos_ref[1:2, :] + f_bwd * pos_ref[2:3, :]) * window
    o_ref[...] = filt.reshape(o_ref.shape)


def _hyena_filter(feats, pos, w1t, w2t, w3t, bfr, w4t, deltas):
    n = feats.shape[1]
    nt = min(n, 2048)
    full = lambda shape: pl.BlockSpec(shape, lambda j: tuple(0 for _ in shape))
    hid = pl.pallas_call(
        _filt_hidden_kernel,
        grid=(n // nt,),
        in_specs=[pl.BlockSpec((HY_FILT_W, nt), lambda j: (0, j)),
                  full((HY_FILT_W, HY_FILT_W)), full((HY_FILT_W, HY_FILT_W)),
                  full((HY_FILT_W, HY_FILT_W)), full((HY_FILT_W, 4))],
        out_specs=pl.BlockSpec((HY_FILT_W, nt), lambda j: (0, j)),
        out_shape=jax.ShapeDtypeStruct((HY_FILT_W, n), F32),
        compiler_params=_cp(("arbitrary",)),
        name="hyena_filter_hidden",
    )(feats, w1t, w2t, w3t, bfr)
    ct = 256
    return pl.pallas_call(
        _filt_kernel,
        grid=(HY_W // ct, n // nt),
        in_specs=[pl.BlockSpec((HY_FILT_W, nt), lambda c, j: (0, j)),
                  pl.BlockSpec((2, ct, HY_FILT_W), lambda c, j: (0, c, 0)),
                  pl.BlockSpec((8, nt), lambda c, j: (0, j)),
                  pl.BlockSpec((ct, 1), lambda c, j: (c, 0))],
        out_specs=pl.BlockSpec((ct, nt // LANES, LANES), lambda c, j: (c, j, 0)),
        out_shape=jax.ShapeDtypeStruct((HY_W, n // LANES, LANES), F32),
        compiler_params=_cp(("parallel", "arbitrary")),
        name="hyena_filter",
    )(hid, w4t, pos, deltas)


def _hyconv_kernel(u_ref, x0_ref, f_ref, skip_ref, g1_ref, g1f_ref, g1i_ref, tw_ref, w3_ref, w3c_ref,
                   o_ref, ub_ref, l3_ref, l3f_ref, bp_ref, *, n1, r, ct):
    twr = tw_ref[0]
    twi = tw_ref[1]
    for bb in range(2):
        ub_ref[bb] = u_ref[bb].astype(F32).reshape(ct, r, LANES)

    for c in range(ct):
        rhs = jnp.concatenate([ub_ref[0, c], ub_ref[1, c]], axis=0).astype(BF)
        bcol = jnp.dot(g1_ref[...], rhs, preferred_element_type=F32)
        br, bi = bcol[:n1], bcol[n1:]
        l3_ref[c * n1:(c + 1) * n1, 0:LANES] = (br * twr - bi * twi).astype(BF)
        l3_ref[c * n1:(c + 1) * n1, LANES:2 * LANES] = (br * twi + bi * twr).astype(BF)
        fcol = jnp.dot(g1f_ref[...], f_ref[c].astype(BF), preferred_element_type=F32)
        fr, fi = fcol[:n1], fcol[n1:]
        l3f_ref[c * n1:(c + 1) * n1, 0:LANES] = (fr * twr - fi * twi).astype(BF)
        l3f_ref[c * n1:(c + 1) * n1, LANES:2 * LANES] = (fr * twi + fi * twr).astype(BF)

    rows_per = 256
    for k in range(ct * n1 // rows_per):
        rows = slice(k * rows_per, (k + 1) * rows_per)
        xs = jnp.dot(l3_ref[rows, :], w3_ref[...], preferred_element_type=F32)
        hs = jnp.dot(l3f_ref[rows, :], w3_ref[...], preferred_element_type=F32)
        xr, xi = xs[:, :LANES], xs[:, LANES:]
        hr, hi = hs[:, :LANES], hs[:, LANES:]
        ys = jnp.concatenate([xr * hr - xi * hi, xr * hi + xi * hr], axis=1).astype(BF)
        bp_ref[rows, :] = jnp.dot(ys, w3c_ref[...], preferred_element_type=F32)

    for c in range(ct):
        bp = bp_ref[c * n1:(c + 1) * n1, :]
        br, bi = bp[:, :LANES], bp[:, LANES:]
        rhs = jnp.concatenate([br * twr + bi * twi, bi * twr - br * twi], axis=0).astype(BF)
        out = jnp.dot(g1i_ref[...], rhs, preferred_element_type=F32)
        ub_ref[0, c] = out[:r]
        ub_ref[1, c] = out[r:]

    skip = skip_ref[...]
    for bb in range(2):
        conv = ub_ref[bb].reshape(ct, r * LANES)
        o_ref[bb] = (x0_ref[bb].astype(F32) * (conv + skip * u_ref[bb].astype(F32))).astype(o_ref.dtype)


@functools.lru_cache(maxsize=None)
def _dft_constants(n1, r):
    n = n1 * LANES
    k1 = np.arange(n1)
    f1 = np.exp(-2j * np.pi * np.outer(k1, k1) / n1)
    f2 = np.exp(-2j * np.pi * np.outer(np.arange(LANES), np.arange(LANES)) / LANES)
    tw = np.exp(-2j * np.pi * np.outer(k1, np.arange(LANES)) / n)
    g1 = np.block([[f1.real[:, :r], -f1.imag[:, :r]], [f1.imag[:, :r], f1.real[:, :r]]])
    g1f = np.concatenate([f1.real, f1.imag], axis=0)
    g1i = np.block([[f1.real[:r], f1.imag[:r]], [-f1.imag[:r], f1.real[:r]]]) / n
    w3 = np.block([[f2.real, f2.imag], [-f2.imag, f2.real]])
    w3c = np.block([[f2.real, -f2.imag], [f2.imag, f2.real]])
    tws = np.stack([tw.real, tw.imag])
    return tuple(np.asarray(a, np.float32) for a in (g1, g1f, g1i, tws, w3, w3c))


def _hyconv(u_t, x0_t, filt, skip, prev, *, n1, r, tok_block):
    b, c_all = u_t.shape[0], u_t.shape[1]
    ct = 16 if n1 >= 128 else 64
    g1, g1f, g1i, tws, w3, w3c = (jnp.asarray(a) for a in _dft_constants(n1, r))
    g1, g1f, g1i, w3, w3c = (a.astype(BF) for a in (g1, g1f, g1i, w3, w3c))
    sig = pl.BlockSpec((b, ct, r * LANES), lambda i: (0, i, tok_block))
    full = lambda a: pl.BlockSpec(a.shape, lambda i: tuple(0 for _ in a.shape))
    args = [u_t, x0_t, filt, skip, g1, g1f, g1i, tws, w3, w3c]
    in_specs = [sig, sig, pl.BlockSpec((ct, n1, LANES), lambda i: (i, 0, 0)),
                pl.BlockSpec((ct, 1), lambda i: (i, 0)),
                full(g1), full(g1f), full(g1i), full(tws), full(w3), full(w3c)]
    aliases = {}
    kern = functools.partial(_hyconv_kernel, n1=n1, r=r, ct=ct)
    if prev is not None:
        args.append(prev)
        in_specs.append(pl.BlockSpec(memory_space=pl.ANY))
        aliases = {len(args) - 1: 0}
        body = kern
        kern = lambda *refs: body(*refs[:10], *refs[11:])
    return pl.pallas_call(
        kern,
        grid=(c_all // ct,),
        in_specs=in_specs,
        out_specs=sig,
        out_shape=jax.ShapeDtypeStruct(u_t.shape, ACT),
        scratch_shapes=[pltpu.VMEM((b, ct, r, LANES), F32),
                        pltpu.VMEM((ct * n1, 2 * LANES), BF), pltpu.VMEM((ct * n1, 2 * LANES), BF),
                        pltpu.VMEM((ct * n1, 2 * LANES), F32)],
        input_output_aliases=aliases,
        compiler_params=_cp(("arbitrary",)),
        name="hyena_conv_n%d" % n1,
    )(*args)


def _merge_kernel(yf_ref, yb_ref, zs_ref, yh_ref, zh_ref, x_ref, mod_ref, gw_ref, w_ref, fw_ref,
                  o_ref, *, final):
    def gated_norm(y, z, gw):
        g = y * z
        ms = jnp.mean(g * g, axis=0, keepdims=True)
        return ((g * lax.rsqrt(ms + EPS)) * gw).astype(BF)

    ms_ = gated_norm(yf_ref[0].astype(F32) + yb_ref[0].astype(F32), zs_ref[0].astype(F32), gw_ref[0:SSD_W, :])
    mh_ = gated_norm(yh_ref[0].astype(F32), zh_ref[0].astype(F32), gw_ref[SSD_W:, :])
    merged = jnp.concatenate([ms_, mh_], axis=0)
    out = jnp.dot(w_ref[...], merged, preferred_element_type=F32)
    xn = x_ref[0] + mod_ref[0, 0, :, 2:3] * out
    if final:
        ms = jnp.mean(xn * xn, axis=0, keepdims=True)
        o_ref[0] = ((xn * lax.rsqrt(ms + EPS)) * fw_ref[...]).T
    else:
        o_ref[0] = xn


def _merge(yf, yb, zs, yh, zh, xt, mod, gw, w_out_t, fw, *, final):
    b = xt.shape[0]
    tiles = LAT_TILES if final else ALL_TILES
    tok = lambda i, t: (i, 0, t)
    const2 = lambda i, t: (0, 0)
    act = lambda rows: pl.BlockSpec((1, rows, TT), tok)
    if final:
        out_spec = pl.BlockSpec((1, TT, D_MODEL), lambda i, t: (i, t, 0))
        out_shape = jax.ShapeDtypeStruct((b, SEQ, D_MODEL), F32)
    else:
        out_spec = act(D_MODEL)
        out_shape = jax.ShapeDtypeStruct((b, D_MODEL, T_ALL), F32)
    return pl.pallas_call(
        functools.partial(_merge_kernel, final=final),
        grid=(b, tiles),
        in_specs=[act(SSD_W), act(SSD_W), act(SSD_W), act(HY_W), act(HY_W), act(D_MODEL),
                  pl.BlockSpec((1, 1, D_MODEL, 4), lambda i, t: (i, jnp.where(t >= LAT_TILES, 1, 0), 0, 0)),
                  pl.BlockSpec((SSD_W + HY_W, 1), const2),
                  pl.BlockSpec((D_MODEL, SSD_W + HY_W), const2),
                  pl.BlockSpec((D_MODEL, 1), const2)],
        out_specs=out_spec,
        out_shape=out_shape,
        compiler_params=_cp(("parallel", "arbitrary")),
        name="merge_out_projection",
    )(yf, yb, zs, yh, zh, xt, mod, gw, w_out_t, fw)


@functools.lru_cache(maxsize=None)
def _conv_masks():
    t = np.arange(T_ALL)
    pos = np.where(t < SEQ, t % GRID_W, t - SEQ)
    period = np.where(t < SEQ, GRID_W, CTX_LEN)
    ml = (pos != 0).astype(np.float32)[None, :]
    mr = (pos != period - 1).astype(np.float32)[None, :]
    return ml, mr


@functools.lru_cache(maxsize=None)
def _filter_positions(n, length):
    idx = np.arange(n)
    fwd = idx < length
    bwd = idx > n - length
    d = np.where(fwd, idx, np.where(bwd, n - idx, 0)).astype(np.float64)
    t = (np.linspace(0.0, 1.0, length, dtype=np.float32).astype(np.float64))[d.astype(np.int64)]
    w = (2.0 * math.pi / length) * d
    f = np.linspace(1e-4, HY_BANDS - 1, HY_BANDS, dtype=np.float32).astype(np.float64)[:, None]
    feats = np.zeros((HY_FILT_W, n), np.float32)
    feats[0] = t
    feats[1:1 + HY_BANDS] = np.cos(f * w[None, :])
    feats[1 + HY_BANDS:HY_EMB] = -np.sin(f * w[None, :])
    pos = np.zeros((8, n), np.float32)
    pos[0] = t
    pos[1] = fwd
    pos[2] = bwd
    return feats, pos


def kernel(x, c, ctx, c_ctx, norm_w, ada_w, ada_b, w_in, ssd_conv_w, ssd_conv_b, dt_bias, a_log, d_skip,
           hy_conv_w, hy_conv_b, filt_w1, filt_b1, filt_w2, filt_b2, filt_w3, filt_b3, filt_w4, filt_freq,
           hy_bias, gnorm_w, w_out, final_norm_w):
    bsz = x.shape[0]
    depth = norm_w.shape[0]
    assert x.shape == (bsz, SEQ, D_MODEL) and ctx.shape == (bsz, CTX_LEN, D_MODEL) and bsz == 2

    s_rows = jnp.zeros((8, D_MODEL), F32).at[:bsz].set(c).at[bsz].set(c_ctx)
    mods = _ada(s_rows, ada_w, ada_b)
    mods = mods[:, :bsz + 1].reshape(depth, bsz + 1, 3, D_MODEL)
    lat = mods[:, :bsz]
    cx = jnp.broadcast_to(mods[:, bsz:bsz + 1], lat.shape)
    mod = jnp.stack([lat, cx], axis=2)
    mod = jnp.stack([mod[:, :, :, 0], 1.0 + mod[:, :, :, 1], mod[:, :, :, 2], jnp.zeros_like(mod[:, :, :, 0])],
                    axis=-1)

    ml, mr = (jnp.asarray(a) for a in _conv_masks())
    deltas = jnp.abs(jnp.linspace(HY_MIN_DECAY, HY_MAX_DECAY, HY_W, dtype=F32))[:, None]
    xt = _to_channel_major(x, ctx)

    out = None
    for l in range(depth):
        last = l == depth - 1
        wl = w_in[l]
        w_t = jnp.concatenate([
            wl[:, COL_ZS:COL_XBC].T, wl[:, COL_XBC:COL_DT].T,
            jnp.pad(wl[:, COL_DT:COL_ZH].T, ((0, DT_ROWS - 2 * SSD_HEADS), (0, 0))),
            wl[:, COL_ZH:COL_HY].T, wl[:, COL_HY:].T], axis=0).astype(BF)
        cw = jnp.concatenate([
            jnp.concatenate([ssd_conv_w[l].T, ssd_conv_b[l][:, None]], axis=1),
            jnp.concatenate([hy_conv_w[l].T, hy_conv_b[l][:, None]], axis=1)], axis=0)
        zs, xbc, dtt, zh, x0, u = _inproj(xt, mod[l], norm_w[l][:, None], w_t, cw, ml, mr)

        par = jnp.stack([dt_bias[l].reshape(-1), a_log[l].reshape(-1)], axis=1)
        dexp = jnp.broadcast_to(jnp.repeat(d_skip[l], SSD_HEADDIM, axis=1)[:, :, None],
                                (2, SSD_W, CHUNK))
        yf, yb = _ssd(xbc, dtt, par, dexp)

        w4t = filt_w4[l].T.reshape(2, HY_W, HY_FILT_W)
        bfr = jnp.stack([filt_b1[l], filt_b2[l], filt_b3[l], filt_freq[l]], axis=1)
        w1t = jnp.pad(filt_w1[l].T, ((0, 0), (0, HY_FILT_W - HY_EMB)))
        skip = hy_bias[l][:, None]

        def long_conv(n1, r, tok_block, length, prev):
            feats, pos = (jnp.asarray(a) for a in _filter_positions(n1 * LANES, length))
            filt = _hyena_filter(feats, pos, w1t, filt_w2[l].T, filt_w3[l].T, bfr, w4t, deltas)
            return _hyconv(u, x0, filt, skip, prev, n1=n1, r=r, tok_block=tok_block)

        yh = long_conv(2 * SEQ // LANES, SEQ // LANES, 0, SEQ, None)
        if not last:
            yh = long_conv(16, HY_CTX_ROWS, SEQ // (HY_CTX_ROWS * LANES), CTX_LEN, yh)

        res = _merge(yf, yb, zs, yh, zh, xt, mod[l], gnorm_w[l][:, None], w_out[l].T.astype(BF),
                     final_norm_w[:, None], final=last)
        if last:
            out = res
        else:
            xt = res
    return out
```

```python
import functools
import math

import numpy as np
import jax
import jax.numpy as jnp
from jax import lax
from jax.experimental import pallas as pl
from jax.experimental.pallas import tpu as pltpu

D_MODEL = 1024
SEQ = 8192
CTX_LEN = 256
GRID_W = 64
SSD_W = 1024
HY_W = 1024
SSD_HEADDIM = 64
SSD_HEADS = SSD_W // SSD_HEADDIM
SSD_GROUPS = 2
SSD_STATE = 128
CHUNK = 128
HY_EMB = 33
HY_BANDS = (HY_EMB - 1) // 2
HY_FILT_W = 64
HY_TARGET = 1e-2
HY_MIN_DECAY = math.log(HY_TARGET) / 1.5
HY_MAX_DECAY = math.log(HY_TARGET) / 0.3
EPS = 1e-6
XBC_W = SSD_W + 2 * SSD_GROUPS * SSD_STATE
COL_ZS = 0
COL_XBC = COL_ZS + SSD_W
COL_DT = COL_XBC + XBC_W
COL_ZH = COL_DT + 2 * SSD_HEADS
COL_HY = COL_ZH + HY_W

LANES = 128
T_ALL = SEQ + CTX_LEN
TT = 256
LAT_TILES = SEQ // TT
ALL_TILES = T_ALL // TT
HY_CTX_ROWS = 8
T_HY = SEQ + HY_CTX_ROWS * LANES
HY_TILES = T_HY // TT
DT_ROWS = 128
DA_ROWS = 5 * 2 * SSD_HEADS
SSD_BLOCK = 2 * CHUNK
R_ZS = 0
R_XBC = R_ZS + SSD_W
R_DT = R_XBC + XBC_W
R_ZH = R_DT + DT_ROWS
R_HY = R_ZH + HY_W
R_END = R_HY + 3 * HY_W
CONV_ROWS = XBC_W + 3 * HY_W

ACT = jnp.bfloat16
BF = jnp.bfloat16
F32 = jnp.float32
HI = lax.Precision.HIGHEST
VMEM_LIMIT = 56 * 1024 * 1024
ROWS_B = 256


def _cp(sem):
    return pltpu.CompilerParams(dimension_semantics=sem, vmem_limit_bytes=VMEM_LIMIT)


def _silu(v):
    return v * jax.nn.sigmoid(v)


def _ada_kernel(s_ref, w_ref, b_ref, o_ref):
    s = _silu(s_ref[...])
    o_ref[0] = jnp.dot(s, w_ref[0], precision=HI, preferred_element_type=F32) + b_ref[0]


def _ada(s_rows, ada_w, ada_b):
    depth, d, n3 = ada_w.shape
    ct = 1024
    return pl.pallas_call(
        _ada_kernel,
        grid=(depth, n3 // ct),
        in_specs=[
            pl.BlockSpec((8, d), lambda l, j: (0, 0)),
            pl.BlockSpec((1, d, ct), lambda l, j: (l, 0, j)),
            pl.BlockSpec((1, 1, ct), lambda l, j: (l, 0, j)),
        ],
        out_specs=pl.BlockSpec((1, 8, ct), lambda l, j: (l, 0, j)),
        out_shape=jax.ShapeDtypeStruct((depth, 8, n3), F32),
        compiler_params=_cp(("arbitrary", "arbitrary")),
        name="ada_modulation",
    )(s_rows, ada_w, ada_b.reshape(depth, 1, n3))


def _inproj_kernel(*refs, row_major):
    if row_major:
        x_ref, ctx_ref, *refs = refs
    else:
        x_ref, *refs = refs
    (mod_ref, nw_ref, w_ref, cw_ref, par_ref, ml_ref, mr_ref,
     zs_ref, xbc_ref, da_ref, cc_ref, zh_ref, x0_ref, u_ref, cwb_ref) = refs
    t = pl.program_id(1)
    rc = 256

    @pl.when(t == 0)
    def _():
        for r in range(0, CONV_ROWS, 512):
            for k in range(4):
                cwb_ref[k, r:r + 512, :] = jnp.broadcast_to(cw_ref[r:r + 512, k:k + 1], (512, LANES))

    @pl.when(t < ALL_TILES)
    def _():
        if row_major:
            x = jnp.where(t < LAT_TILES, x_ref[0], ctx_ref[0])
            ms = jnp.mean(x * x, axis=1, keepdims=True)
            xn = x * lax.rsqrt(ms + EPS)
            xb = ((xn * nw_ref[...]) * mod_ref[0, 0, 1:2, :] + mod_ref[0, 0, 0:1, :]).astype(BF)
            dims = (((1,), (1,)), ((), ()))
        else:
            x = x_ref[0]
            ms = jnp.mean(x * x, axis=0, keepdims=True)
            xn = x * lax.rsqrt(ms + EPS)
            xb = ((xn * nw_ref[...]) * mod_ref[0, 0, :, 1:2] + mod_ref[0, 0, :, 0:1]).astype(BF)
            dims = (((1,), (0,)), ((), ()))
        ml = ml_ref[...]
        mr = mr_ref[...]

        def mm(r0, rows):
            return lax.dot_general(w_ref[r0:r0 + rows, :], xb, dims, preferred_element_type=F32)

        def conv(p, r0, rows):
            left = pltpu.roll(p, 1, axis=1) * ml
            right = pltpu.roll(p, TT - 1, axis=1) * mr
            rs = slice(r0, r0 + rows)
            halves = []
            for hl in range(0, TT, LANES):
                ls = slice(hl, hl + LANES)
                halves.append(cwb_ref[0, rs, :] * left[:, ls] + cwb_ref[1, rs, :] * p[:, ls]
                              + cwb_ref[2, rs, :] * right[:, ls] + cwb_ref[3, rs, :])
            return jnp.concatenate(halves, axis=1)

        for r in range(0, SSD_W, rc):
            zs_ref[0, r:r + rc, :] = _silu(mm(R_ZS + r, rc)).astype(zs_ref.dtype)
        for r in range(0, XBC_W, rc):
            xbc_ref[0, r:r + rc, :] = _silu(conv(mm(R_XBC + r, rc), r, rc)).astype(xbc_ref.dtype)
        nh = SSD_HEADS
        dt = jax.nn.softplus(mm(R_DT, DT_ROWS)[0:2 * nh, :] + par_ref[:, 0:1])
        a = dt * -jnp.exp(par_ref[:, 1:2])
        pos = lax.broadcasted_iota(jnp.int32, a.shape, 1) & (CHUNK - 1)
        pre = a
        suf = a
        k = 1
        while k < CHUNK:
            pre = pre + jnp.where(pos >= k, pltpu.roll(pre, k, axis=1), 0.0)
            suf = suf + jnp.where(pos < CHUNK - k, pltpu.roll(suf, TT - k, axis=1), 0.0)
            k *= 2
        tot = pre + suf - a
        cum = jnp.concatenate([pre[0:nh], suf[nh:2 * nh]], axis=0)
        da_ref[0] = jnp.concatenate([dt, cum, jnp.exp(tot - cum), jnp.exp(cum), jnp.exp(tot)], axis=0)
        for hc in range(TT // CHUNK):
            sq = jnp.concatenate([cum[:, hc * CHUNK:(hc + 1) * CHUNK],
                                  jnp.zeros((LANES - 2 * nh, CHUNK), F32)], axis=0)
            cc_ref[0, hc * CHUNK:(hc + 1) * CHUNK, :] = sq.T
        for r in range(0, HY_W, rc):
            zh_ref[0, r:r + rc, :] = _silu(mm(R_ZH + r, rc)).astype(zh_ref.dtype)
        for r in range(0, HY_W, rc):
            p0 = conv(mm(R_HY + r, rc), XBC_W + r, rc)
            p1 = conv(mm(R_HY + HY_W + r, rc), XBC_W + HY_W + r, rc)
            pv = conv(mm(R_HY + 2 * HY_W + r, rc), XBC_W + 2 * HY_W + r, rc)
            x0_ref[0, r:r + rc, :] = p0.astype(x0_ref.dtype)
            u_ref[0, r:r + rc, :] = (p1 * pv).astype(u_ref.dtype)

    @pl.when(t >= ALL_TILES)
    def _():
        x0_ref[...] = jnp.zeros_like(x0_ref)
        u_ref[...] = jnp.zeros_like(u_ref)


def _mod_kind(i, t):
    return (i, jnp.where(t >= LAT_TILES, 1, 0), 0, 0)


def _stream_specs(row_major):
    if row_major:
        return [pl.BlockSpec((1, TT, D_MODEL), lambda i, t: (i, jnp.minimum(t, LAT_TILES - 1), 0)),
                pl.BlockSpec((1, CTX_LEN, D_MODEL), lambda i, t: (i, 0, 0))]
    return [pl.BlockSpec((1, D_MODEL, TT), lambda i, t: (i, 0, jnp.minimum(t, ALL_TILES - 1)))]


def _inproj(stream, mod, nw, w_t, cw, par, ml, mr, *, row_major):
    b = stream[0].shape[0]
    vec_specs = [pl.BlockSpec((1, 1) + mod.shape[2:], _mod_kind), pl.BlockSpec(nw.shape, lambda i, t: (0, 0))]
    last = ALL_TILES - 1
    tok = lambda i, t: (i, 0, jnp.minimum(t, last))
    const2 = lambda i, t: (0, 0)
    out_tok = lambda rows: pl.BlockSpec((1, rows, TT), tok)
    return pl.pallas_call(
        functools.partial(_inproj_kernel, row_major=row_major),
        grid=(b, HY_TILES),
        in_specs=_stream_specs(row_major) + vec_specs + [
            pl.BlockSpec((R_END, D_MODEL), const2, pipeline_mode=pl.Buffered(1)),
            pl.BlockSpec((CONV_ROWS, 4), const2),
            pl.BlockSpec((2 * SSD_HEADS, 2), const2),
            pl.BlockSpec((1, TT), lambda i, t: (0, jnp.minimum(t, last))),
            pl.BlockSpec((1, TT), lambda i, t: (0, jnp.minimum(t, last))),
        ],
        out_specs=[
            out_tok(SSD_W), out_tok(XBC_W), out_tok(DA_ROWS),
            pl.BlockSpec((1, TT, LANES), lambda i, t: (i, jnp.minimum(t, last), 0)),
            out_tok(HY_W),
            pl.BlockSpec((1, HY_W, TT), lambda i, t: (i, 0, t)),
            pl.BlockSpec((1, HY_W, TT), lambda i, t: (i, 0, t)),
        ],
        out_shape=[
            jax.ShapeDtypeStruct((b, SSD_W, T_ALL), ACT),
            jax.ShapeDtypeStruct((b, XBC_W, T_ALL), ACT),
            jax.ShapeDtypeStruct((b, DA_ROWS, T_ALL), F32),
            jax.ShapeDtypeStruct((b, T_ALL, LANES), F32),
            jax.ShapeDtypeStruct((b, HY_W, T_ALL), ACT),
            jax.ShapeDtypeStruct((b, HY_W, T_HY), ACT),
            jax.ShapeDtypeStruct((b, HY_W, T_HY), ACT),
        ],
        scratch_shapes=[pltpu.VMEM((4, CONV_ROWS, LANES), F32)],
        compiler_params=_cp(("parallel", "arbitrary")),
        name="in_projection",
    )(*stream, mod, nw, w_t, cw, par, ml, mr)


def _ssd_direction(d, ck, x_ref, da_ref, cc_ref, dexp_ref, y_ref, st_ref, xdt_ref, w_ref, valid):
    nh = SSD_HEADS
    hd = SSD_HEADDIM
    hpg = nh // SSD_GROUPS
    o = d * nh
    tk = slice(ck * CHUNK, (ck + 1) * CHUNK)
    field = lambda k: da_ref[0, 2 * k * nh + o:2 * k * nh + o + nh, tk]
    dt, cum_t, e_end, e_in, e_tot = (field(k) for k in range(5))
    cum_c = cc_ref[0, tk, o:o + nh]
    for h in range(nh):
        rows = slice(h * hd, (h + 1) * hd)
        xdt = x_ref[0, rows, tk].astype(F32) * dt[h:h + 1, :]
        xdt_ref[rows, :] = xdt.astype(BF)
        w_ref[rows, :] = (xdt * e_end[h:h + 1, :]).astype(BF)
    for g in range(SSD_GROUPS):
        grows = slice(g * hpg * hd, (g + 1) * hpg * hd)
        b_t = x_ref[0, SSD_W + g * SSD_STATE:SSD_W + (g + 1) * SSD_STATE, tk].astype(BF)
        c_t = x_ref[0, SSD_W + (SSD_GROUPS + g) * SSD_STATE:
                    SSD_W + (SSD_GROUPS + g + 1) * SSD_STATE, tk].astype(BF)
        cb_t = lax.dot_general(b_t, c_t, (((0,), (0,)), ((), ())), preferred_element_type=F32)
        inter = jnp.dot(st_ref[grows, :].astype(BF), c_t, preferred_element_type=F32)
        s_new = lax.dot_general(w_ref[grows, :], b_t, (((1,), (1,)), ((), ())),
                                preferred_element_type=F32)
        for h in range(g * hpg, (g + 1) * hpg):
            rows = slice(h * hd, (h + 1) * hd)
            loc = slice((h - g * hpg) * hd, (h - g * hpg + 1) * hd)
            seg = cum_t[h:h + 1, :] - cum_c[:, h:h + 1]
            m_t = (cb_t * jnp.exp(jnp.where(valid, seg, -jnp.inf))).astype(BF)
            intra = jnp.dot(xdt_ref[rows, :], m_t, preferred_element_type=F32)
            y_ref[0, rows, tk] = (intra + inter[loc] * e_in[h:h + 1, :]
                                  + dexp_ref[d, rows, :] * x_ref[0, rows, tk].astype(F32)).astype(y_ref.dtype)
            st_ref[rows, :] = e_tot[h:h + 1, :] * st_ref[rows, :] + s_new[loc]


def _ssd_kernel(xf_ref, xb_ref, daf_ref, dab_ref, ccf_ref, ccb_ref, dexp_ref, yf_ref, yb_ref,
                sf_ref, sb_ref, xdtf_ref, wf_ref, xdtb_ref, wb_ref):
    @pl.when(pl.program_id(1) == 0)
    def _():
        sf_ref[...] = jnp.zeros_like(sf_ref)
        sb_ref[...] = jnp.zeros_like(sb_ref)

    row = lax.broadcasted_iota(jnp.int32, (CHUNK, CHUNK), 0)
    col = lax.broadcasted_iota(jnp.int32, (CHUNK, CHUNK), 1)
    per = SSD_BLOCK // CHUNK
    for k in range(per):
        _ssd_direction(0, k, xf_ref, daf_ref, ccf_ref, dexp_ref, yf_ref, sf_ref, xdtf_ref, wf_ref,
                       row <= col)
        _ssd_direction(1, per - 1 - k, xb_ref, dab_ref, ccb_ref, dexp_ref, yb_ref, sb_ref, xdtb_ref, wb_ref,
                       row >= col)


def _ssd(xbc_t, da_t, cc, dexp):
    b = xbc_t.shape[0]
    nb = T_ALL // SSD_BLOCK
    lat = SEQ // SSD_BLOCK
    fwd = lambda i, s: (i, 0, (s + lat) % nb)
    bwd = lambda i, s: (i, 0, nb - 1 - s)
    fwd_r = lambda i, s: (i, (s + lat) % nb, 0)
    bwd_r = lambda i, s: (i, nb - 1 - s, 0)
    return pl.pallas_call(
        _ssd_kernel,
        grid=(b, nb),
        in_specs=[
            pl.BlockSpec((1, XBC_W, SSD_BLOCK), fwd),
            pl.BlockSpec((1, XBC_W, SSD_BLOCK), bwd),
            pl.BlockSpec((1, DA_ROWS, SSD_BLOCK), fwd),
            pl.BlockSpec((1, DA_ROWS, SSD_BLOCK), bwd),
            pl.BlockSpec((1, SSD_BLOCK, LANES), fwd_r),
            pl.BlockSpec((1, SSD_BLOCK, LANES), bwd_r),
            pl.BlockSpec((2, SSD_W, CHUNK), lambda i, s: (0, 0, 0)),
        ],
        out_specs=[pl.BlockSpec((1, SSD_W, SSD_BLOCK), fwd), pl.BlockSpec((1, SSD_W, SSD_BLOCK), bwd)],
        out_shape=[jax.ShapeDtypeStruct((b, SSD_W, T_ALL), ACT)] * 2,
        scratch_shapes=[pltpu.VMEM((SSD_W, SSD_STATE), F32)] * 2 + [pltpu.VMEM((SSD_W, CHUNK), BF)] * 4,
        compiler_params=_cp(("parallel", "arbitrary")),
        name="ssd_scan",
    )(xbc_t, xbc_t, da_t, da_t, cc, cc, dexp)


def _split_bf16(v):
    hi = v.astype(BF)
    return hi, (v - hi.astype(F32)).astype(BF)


def _dot3(a, b):
    a_hi, a_lo = _split_bf16(a)
    b_hi, b_lo = _split_bf16(b)
    return jnp.dot(jnp.concatenate([a_hi, a_hi, a_lo], axis=1),
                   jnp.concatenate([b_hi, b_lo, b_hi], axis=0), preferred_element_type=F32)


def _filt_hidden_kernel(f_ref, w1_ref, w2_ref, w3_ref, bf_ref, o_ref):
    fr = bf_ref[:, 3:4]
    hcur = jnp.sin(fr * (_dot3(w1_ref[...], f_ref[...]) + bf_ref[:, 0:1]))
    hcur = jnp.sin(fr * (_dot3(w2_ref[...], hcur) + bf_ref[:, 1:2]))
    o_ref[...] = jnp.sin(fr * (_dot3(w3_ref[...], hcur) + bf_ref[:, 2:3]))


def _filt_kernel(h_ref, w4_ref, pos_ref, dl_ref, o_ref, *, both):
    hid = h_ref[...]
    window = jnp.exp(-dl_ref[...] * pos_ref[0:1, :])
    if both:
        filt = _dot3(w4_ref[0], hid) * pos_ref[1:2, :] + _dot3(w4_ref[1], hid) * pos_ref[2:3, :]
    else:
        filt = _dot3(w4_ref[0], hid) * (pos_ref[1:2, :] + pos_ref[2:3, :])
    o_ref[...] = (filt * window).reshape(o_ref.shape)


def _hyena_filter(feats, pos, length, w1t, w2t, w3t, bfr, w4t, deltas):
    n = feats.shape[1]
    nt = min(n, 2048)
    full = lambda shape: pl.BlockSpec(shape, lambda j: tuple(0 for _ in shape))
    hid = pl.pallas_call(
        _filt_hidden_kernel,
        grid=(n // nt,),
        in_specs=[pl.BlockSpec((HY_FILT_W, nt), lambda j: (0, j)),
                  full((HY_FILT_W, HY_FILT_W)), full((HY_FILT_W, HY_FILT_W)),
                  full((HY_FILT_W, HY_FILT_W)), full((HY_FILT_W, 4))],
        out_specs=pl.BlockSpec((HY_FILT_W, nt), lambda j: (0, j)),
        out_shape=jax.ShapeDtypeStruct((HY_FILT_W, n), F32),
        compiler_params=_cp(("arbitrary",)),
        name="hyena_filter_hidden",
    )(feats, w1t, w2t, w3t, bfr)
    ct = 256
    tiles = n // nt
    both = tiles == 1
    assert both or n == 2 * length
    w4_spec = (pl.BlockSpec((2, ct, HY_FILT_W), lambda c, j: (0, c, 0)) if both else
               pl.BlockSpec((1, ct, HY_FILT_W), lambda c, j: (j // (tiles // 2), c, 0)))
    return pl.pallas_call(
        functools.partial(_filt_kernel, both=both),
        grid=(HY_W // ct, tiles),
        in_specs=[pl.BlockSpec((HY_FILT_W, nt), lambda c, j: (0, j)),
                  w4_spec,
                  pl.BlockSpec((8, nt), lambda c, j: (0, j)),
                  pl.BlockSpec((ct, 1), lambda c, j: (c, 0))],
        out_specs=pl.BlockSpec((ct, nt // LANES, LANES), lambda c, j: (c, j, 0)),
        out_shape=jax.ShapeDtypeStruct((HY_W, n // LANES, LANES), F32),
        compiler_params=_cp(("parallel", "arbitrary")),
        name="hyena_filter",
    )(hid, w4t, pos, deltas)


def _hyconv_kernel(u_ref, x0_ref, f_ref, skip_ref, g1_ref, g1f_ref, g1i_ref, tw_ref, w3_ref, w3c_ref,
                   o_ref, ub_ref, l3_ref, l3f_ref, bp_ref, *, n1, r, ct):
    twr = tw_ref[0]
    twi = tw_ref[1]
    for bb in range(2):
        ub_ref[bb] = u_ref[bb].astype(F32).reshape(ct, r, LANES)

    def twiddled_rows(dst_ref, col, c):
        cr, ci = col[:n1], col[n1:]
        dst_ref[c * n1:(c + 1) * n1, 0:LANES] = (cr * twr - ci * twi).astype(BF)
        dst_ref[c * n1:(c + 1) * n1, LANES:2 * LANES] = (cr * twi + ci * twr).astype(BF)

    for c in range(0, ct, 2):
        rhs = jnp.concatenate(
            [jnp.concatenate([ub_ref[0, c + j], ub_ref[1, c + j]], axis=0) for j in range(2)],
            axis=1).astype(BF)
        bcol = jnp.dot(g1_ref[...], rhs, preferred_element_type=F32)
        frhs = jnp.concatenate([f_ref[c], f_ref[c + 1]], axis=1).astype(BF)
        fcol = jnp.dot(g1f_ref[...], frhs, preferred_element_type=F32)
        for j in range(2):
            twiddled_rows(l3_ref, bcol[:, j * LANES:(j + 1) * LANES], c + j)
            twiddled_rows(l3f_ref, fcol[:, j * LANES:(j + 1) * LANES], c + j)

    rows_per = ROWS_B
    for k in range(ct * n1 // rows_per):
        rows = slice(k * rows_per, (k + 1) * rows_per)
        xs = jnp.dot(l3_ref[rows, :], w3_ref[...], preferred_element_type=F32)
        hs = jnp.dot(l3f_ref[rows, :], w3_ref[...], preferred_element_type=F32)
        xr, xi = xs[:, :LANES], xs[:, LANES:]
        hr, hi = hs[:, :LANES], hs[:, LANES:]
        ys = jnp.concatenate([xr * hr - xi * hi, xr * hi + xi * hr], axis=1).astype(BF)
        bp_ref[rows, :] = jnp.dot(ys, w3c_ref[...], preferred_element_type=F32)

    def untwiddled(c):
        bp = bp_ref[c * n1:(c + 1) * n1, :]
        br, bi = bp[:, :LANES], bp[:, LANES:]
        return jnp.concatenate([br * twr + bi * twi, bi * twr - br * twi], axis=0)

    for c in range(0, ct, 2):
        rhs = jnp.concatenate([untwiddled(c), untwiddled(c + 1)], axis=1).astype(BF)
        out = jnp.dot(g1i_ref[...], rhs, preferred_element_type=F32)
        for j in range(2):
            ub_ref[0, c + j] = out[:r, j * LANES:(j + 1) * LANES]
            ub_ref[1, c + j] = out[r:, j * LANES:(j + 1) * LANES]

    skip = skip_ref[...]
    for bb in range(2):
        conv = ub_ref[bb].reshape(ct, r * LANES)
        o_ref[bb] = (x0_ref[bb].astype(F32) * (conv + skip * u_ref[bb].astype(F32))).astype(o_ref.dtype)


@functools.lru_cache(maxsize=None)
def _dft_constants(n1, r):
    n = n1 * LANES
    k1 = np.arange(n1)
    f1 = np.exp(-2j * np.pi * np.outer(k1, k1) / n1)
    f2 = np.exp(-2j * np.pi * np.outer(np.arange(LANES), np.arange(LANES)) / LANES)
    tw = np.exp(-2j * np.pi * np.outer(k1, np.arange(LANES)) / n)
    g1 = np.block([[f1.real[:, :r], -f1.imag[:, :r]], [f1.imag[:, :r], f1.real[:, :r]]])
    g1f = np.concatenate([f1.real, f1.imag], axis=0)
    g1i = np.block([[f1.real[:r], f1.imag[:r]], [-f1.imag[:r], f1.real[:r]]]) / n
    w3 = np.block([[f2.real, f2.imag], [-f2.imag, f2.real]])
    w3c = np.block([[f2.real, -f2.imag], [f2.imag, f2.real]])
    tws = np.stack([tw.real, tw.imag])
    return tuple(np.asarray(a, np.float32) for a in (g1, g1f, g1i, tws, w3, w3c))


def _hyconv(u_t, x0_t, filt, skip, prev, *, n1, r, tok_block):
    b, c_all = u_t.shape[0], u_t.shape[1]
    ct = 16 if n1 >= 128 else 64
    g1, g1f, g1i, tws, w3, w3c = (jnp.asarray(a) for a in _dft_constants(n1, r))
    g1, g1f, g1i, w3, w3c = (a.astype(BF) for a in (g1, g1f, g1i, w3, w3c))
    sig = pl.BlockSpec((b, ct, r * LANES), lambda i: (0, i, tok_block))
    full = lambda a: pl.BlockSpec(a.shape, lambda i: tuple(0 for _ in a.shape))
    args = [u_t, x0_t, filt, skip, g1, g1f, g1i, tws, w3, w3c]
    in_specs = [sig, sig, pl.BlockSpec((ct, n1, LANES), lambda i: (i, 0, 0)),
                pl.BlockSpec((ct, 1), lambda i: (i, 0)),
                full(g1), full(g1f), full(g1i), full(tws), full(w3), full(w3c)]
    aliases = {}
    kern = functools.partial(_hyconv_kernel, n1=n1, r=r, ct=ct)
    if prev is not None:
        args.append(prev)
        in_specs.append(pl.BlockSpec(memory_space=pl.ANY))
        aliases = {len(args) - 1: 0}
        body = kern
        kern = lambda *refs: body(*refs[:10], *refs[11:])
    return pl.pallas_call(
        kern,
        grid=(c_all // ct,),
        in_specs=in_specs,
        out_specs=sig,
        out_shape=jax.ShapeDtypeStruct(u_t.shape, ACT),
        scratch_shapes=[pltpu.VMEM((b, ct, r, LANES), F32),
                        pltpu.VMEM((ct * n1, 2 * LANES), BF), pltpu.VMEM((ct * n1, 2 * LANES), BF),
                        pltpu.VMEM((ct * n1, 2 * LANES), F32)],
        input_output_aliases=aliases,
        compiler_params=_cp(("arbitrary",)),
        name="hyena_conv_n%d" % n1,
    )(*args)


def _merge_kernel(*refs, final, row_major):
    yf_ref, yb_ref, zs_ref, yh_ref, zh_ref, *refs = refs
    if row_major:
        x_ref, ctx_ref, mod_ref, gw_ref, w_ref, fw_ref, o_ref = refs
        x = jnp.where(pl.program_id(1) < LAT_TILES, x_ref[0], ctx_ref[0]).T
    else:
        x_ref, mod_ref, gw_ref, w_ref, fw_ref, o_ref = refs
        x = x_ref[0]
    gate = mod_ref[0, 0, :, 2:3]

    def gated_norm(y, z, gw):
        g = y * z
        ms = jnp.mean(g * g, axis=0, keepdims=True)
        return ((g * lax.rsqrt(ms + EPS)) * gw).astype(BF)

    ms_ = gated_norm(yf_ref[0].astype(F32) + yb_ref[0].astype(F32), zs_ref[0].astype(F32), gw_ref[0:SSD_W, :])
    mh_ = gated_norm(yh_ref[0].astype(F32), zh_ref[0].astype(F32), gw_ref[SSD_W:, :])
    merged = jnp.concatenate([ms_, mh_], axis=0)
    out = jnp.dot(w_ref[...], merged, preferred_element_type=F32)
    xn = x + gate * out
    if final:
        ms = jnp.mean(xn * xn, axis=0, keepdims=True)
        o_ref[0] = ((xn * lax.rsqrt(ms + EPS)) * fw_ref[...]).T
    else:
        o_ref[0] = xn


def _merge(yf, yb, zs, yh, zh, stream, mod, gw, w_out_t, fw, *, final, row_major):
    b = yf.shape[0]
    tiles = LAT_TILES if final else ALL_TILES
    tok = lambda i, t: (i, 0, t)
    const2 = lambda i, t: (0, 0)
    act = lambda rows: pl.BlockSpec((1, rows, TT), tok)
    if final:
        out_spec = pl.BlockSpec((1, TT, D_MODEL), lambda i, t: (i, t, 0))
        out_shape = jax.ShapeDtypeStruct((b, SEQ, D_MODEL), F32)
    else:
        out_spec = act(D_MODEL)
        out_shape = jax.ShapeDtypeStruct((b, D_MODEL, T_ALL), F32)
    return pl.pallas_call(
        functools.partial(_merge_kernel, final=final, row_major=row_major),
        grid=(b, tiles),
        in_specs=[act(SSD_W), act(SSD_W), act(SSD_W), act(HY_W), act(HY_W)] + _stream_specs(row_major) + [
                  pl.BlockSpec((1, 1, D_MODEL, 4), _mod_kind),
                  pl.BlockSpec((SSD_W + HY_W, 1), const2),
                  pl.BlockSpec((D_MODEL, SSD_W + HY_W), const2),
                  pl.BlockSpec((D_MODEL, 1), const2)],
        out_specs=out_spec,
        out_shape=out_shape,
        compiler_params=_cp(("parallel", "arbitrary")),
        name="merge_out_projection",
    )(yf, yb, zs, yh, zh, *stream, mod, gw, w_out_t, fw)


@functools.lru_cache(maxsize=None)
def _conv_masks():
    t = np.arange(T_ALL)
    pos = np.where(t < SEQ, t % GRID_W, t - SEQ)
    period = np.where(t < SEQ, GRID_W, CTX_LEN)
    ml = (pos != 0).astype(np.float32)[None, :]
    mr = (pos != period - 1).astype(np.float32)[None, :]
    return ml, mr


@functools.lru_cache(maxsize=None)
def _filter_positions(n, length):
    idx = np.arange(n)
    fwd = idx < length
    bwd = idx > n - length
    d = np.where(fwd, idx, np.where(bwd, n - idx, 0)).astype(np.float64)
    t = (np.linspace(0.0, 1.0, length, dtype=np.float32).astype(np.float64))[d.astype(np.int64)]
    w = (2.0 * math.pi / length) * d
    f = np.linspace(1e-4, HY_BANDS - 1, HY_BANDS, dtype=np.float32).astype(np.float64)[:, None]
    feats = np.zeros((HY_FILT_W, n), np.float32)
    feats[0] = t
    feats[1:1 + HY_BANDS] = np.cos(f * w[None, :])
    feats[1 + HY_BANDS:HY_EMB] = -np.sin(f * w[None, :])
    pos = np.zeros((8, n), np.float32)
    pos[0] = t
    pos[1] = fwd
    pos[2] = bwd
    return feats, pos


def kernel(x, c, ctx, c_ctx, norm_w, ada_w, ada_b, w_in, ssd_conv_w, ssd_conv_b, dt_bias, a_log, d_skip,
           hy_conv_w, hy_conv_b, filt_w1, filt_b1, filt_w2, filt_b2, filt_w3, filt_b3, filt_w4, filt_freq,
           hy_bias, gnorm_w, w_out, final_norm_w):
    bsz = x.shape[0]
    depth = norm_w.shape[0]
    assert x.shape == (bsz, SEQ, D_MODEL) and ctx.shape == (bsz, CTX_LEN, D_MODEL) and bsz == 2

    s_rows = jnp.zeros((8, D_MODEL), F32).at[:bsz].set(c).at[bsz].set(c_ctx)
    mods = _ada(s_rows, ada_w, ada_b)
    mods = mods[:, :bsz + 1].reshape(depth, bsz + 1, 3, D_MODEL)
    lat = mods[:, :bsz]
    cx = jnp.broadcast_to(mods[:, bsz:bsz + 1], lat.shape)
    mod = jnp.stack([lat, cx], axis=2)
    mod = jnp.stack([mod[:, :, :, 0], 1.0 + mod[:, :, :, 1], mod[:, :, :, 2], jnp.zeros_like(mod[:, :, :, 0])],
                    axis=-1)
    mod_rows = jnp.swapaxes(mod, -1, -2)

    ml, mr = (jnp.asarray(a) for a in _conv_masks())
    deltas = jnp.abs(jnp.linspace(HY_MIN_DECAY, HY_MAX_DECAY, HY_W, dtype=F32))[:, None]

    stream = (x, ctx)
    out = None
    for l in range(depth):
        last = l == depth - 1
        row_major = l == 0
        wl = w_in[l]
        w_t = jnp.concatenate([
            wl[:, COL_ZS:COL_XBC].T, wl[:, COL_XBC:COL_DT].T,
            jnp.pad(wl[:, COL_DT:COL_ZH].T, ((0, DT_ROWS - 2 * SSD_HEADS), (0, 0))),
            wl[:, COL_ZH:COL_HY].T, wl[:, COL_HY:].T], axis=0).astype(BF)
        cw = jnp.concatenate([
            jnp.concatenate([ssd_conv_w[l].T, ssd_conv_b[l][:, None]], axis=1),
            jnp.concatenate([hy_conv_w[l].T, hy_conv_b[l][:, None]], axis=1)], axis=0)
        par = jnp.stack([dt_bias[l].reshape(-1), a_log[l].reshape(-1)], axis=1)
        if row_major:
            vecs = (mod_rows[l], norm_w[l][None, :])
        else:
            vecs = (mod[l], norm_w[l][:, None])
        zs, xbc, da, cc, zh, x0, u = _inproj(stream, *vecs, w_t, cw, par, ml, mr, row_major=row_major)

        dexp = jnp.broadcast_to(jnp.repeat(d_skip[l], SSD_HEADDIM, axis=1)[:, :, None],
                                (2, SSD_W, CHUNK))
        yf, yb = _ssd(xbc, da, cc, dexp)

        w4t = filt_w4[l].T.reshape(2, HY_W, HY_FILT_W)
        bfr = jnp.stack([filt_b1[l], filt_b2[l], filt_b3[l], filt_freq[l]], axis=1)
        w1t = jnp.pad(filt_w1[l].T, ((0, 0), (0, HY_FILT_W - HY_EMB)))
        skip = hy_bias[l][:, None]

        def long_conv(n1, r, tok_block, length, prev):
            feats, pos = (jnp.asarray(a) for a in _filter_positions(n1 * LANES, length))
            filt = _hyena_filter(feats, pos, length, w1t, filt_w2[l].T, filt_w3[l].T, bfr, w4t, deltas)
            return _hyconv(u, x0, filt, skip, prev, n1=n1, r=r, tok_block=tok_block)

        yh = long_conv(2 * SEQ // LANES, SEQ // LANES, 0, SEQ, None)
        if not last:
            yh = long_conv(16, HY_CTX_ROWS, SEQ // (HY_CTX_ROWS * LANES), CTX_LEN, yh)

        res = _merge(yf, yb, zs, yh, zh, stream, mod[l], gnorm_w[l][:, None], w_out[l].T.astype(BF),
                     final_norm_w[:, None], final=last, row_major=row_major)
        if last:
            out = res
        else:
            stream = (res,)
    return out
```

```python
import functools
import math

import numpy as np
import jax
import jax.numpy as jnp
from jax import lax
from jax.experimental import pallas as pl
from jax.experimental.pallas import tpu as pltpu

D_MODEL = 1024
SEQ = 8192
CTX_LEN = 256
GRID_W = 64
SSD_W = 1024
HY_W = 1024
SSD_HEADDIM = 64
SSD_HEADS = SSD_W // SSD_HEADDIM
SSD_GROUPS = 2
SSD_STATE = 128
CHUNK = 128
HY_EMB = 33
HY_BANDS = (HY_EMB - 1) // 2
HY_FILT_W = 64
HY_TARGET = 1e-2
HY_MIN_DECAY = math.log(HY_TARGET) / 1.5
HY_MAX_DECAY = math.log(HY_TARGET) / 0.3
EPS = 1e-6
XBC_W = SSD_W + 2 * SSD_GROUPS * SSD_STATE
COL_ZS = 0
COL_XBC = COL_ZS + SSD_W
COL_DT = COL_XBC + XBC_W
COL_ZH = COL_DT + 2 * SSD_HEADS
COL_HY = COL_ZH + HY_W

LANES = 128
T_ALL = SEQ + CTX_LEN
TT = 256
LAT_TILES = SEQ // TT
ALL_TILES = T_ALL // TT
HY_CTX_ROWS = 8
T_HY = SEQ + HY_CTX_ROWS * LANES
HY_TILES = T_HY // TT
DT_ROWS = 128
DA_ROWS = 5 * 2 * SSD_HEADS
SSD_BLOCK = 2 * CHUNK
R_ZS = COL_ZS
R_XBC = COL_XBC
R_DT = COL_DT
R_ZH = COL_ZH
R_HY = COL_HY
R_END = COL_HY + 3 * HY_W
CONV_ROWS = XBC_W + 3 * HY_W

ACT = jnp.bfloat16
BF = jnp.bfloat16
F32 = jnp.float32
HI = lax.Precision.HIGHEST
VMEM_LIMIT = 56 * 1024 * 1024


def _cp(sem, flags=None):
    return pltpu.CompilerParams(dimension_semantics=sem, vmem_limit_bytes=VMEM_LIMIT, flags=flags)


def _silu(v):
    return v * jax.nn.sigmoid(v)


def _ada_kernel(s_ref, w_ref, b_ref, o_ref):
    s = _silu(s_ref[...])
    o_ref[0] = jnp.dot(s, w_ref[0], precision=HI, preferred_element_type=F32) + b_ref[0]


def _ada(s_rows, ada_w, ada_b):
    depth, d, n3 = ada_w.shape
    ct = 1024
    return pl.pallas_call(
        _ada_kernel,
        grid=(depth, n3 // ct),
        in_specs=[
            pl.BlockSpec((8, d), lambda l, j: (0, 0)),
            pl.BlockSpec((1, d, ct), lambda l, j: (l, 0, j)),
            pl.BlockSpec((1, 1, ct), lambda l, j: (l, 0, j)),
        ],
        out_specs=pl.BlockSpec((1, 8, ct), lambda l, j: (l, 0, j)),
        out_shape=jax.ShapeDtypeStruct((depth, 8, n3), F32),
        compiler_params=_cp(("arbitrary", "arbitrary")),
        name="ada_modulation",
    )(s_rows, ada_w, ada_b.reshape(depth, 1, n3))


def _inproj_kernel(*refs, row_major):
    if row_major:
        x_ref, ctx_ref, *refs = refs
    else:
        x_ref, *refs = refs
    (mod_ref, nw_ref, w_ref, cw_ref, par_ref, ml_ref, mr_ref,
     zs_ref, xbc_ref, da_ref, cc_ref, zh_ref, x0_ref, u_ref, cwb_ref) = refs
    t = pl.program_id(1)
    rc = 256

    @pl.when(t == 0)
    def _():
        for r in range(0, CONV_ROWS, 512):
            for k in range(4):
                cwb_ref[k, r:r + 512, :] = jnp.broadcast_to(cw_ref[r:r + 512, k:k + 1], (512, LANES))

    @pl.when(t < ALL_TILES)
    def _():
        if row_major:
            x = jnp.where(t < LAT_TILES, x_ref[0], ctx_ref[0])
            ms = jnp.mean(x * x, axis=1, keepdims=True)
            xn = x * lax.rsqrt(ms + EPS)
            xb = ((xn * nw_ref[...]) * mod_ref[0, 0, 1:2, :] + mod_ref[0, 0, 0:1, :]).astype(BF)
            dims = (((1,), (1,)), ((), ()))
        else:
            x = x_ref[0]
            ms = jnp.mean(x * x, axis=0, keepdims=True)
            xn = x * lax.rsqrt(ms + EPS)
            xb = ((xn * nw_ref[...]) * mod_ref[0, 0, :, 1:2] + mod_ref[0, 0, :, 0:1]).astype(BF)
            dims = (((1,), (0,)), ((), ()))
        ml = ml_ref[...]
        mr = mr_ref[...]

        def mm(r0, rows):
            return lax.dot_general(w_ref[r0:r0 + rows, :], xb, dims, preferred_element_type=F32)

        def conv(p, r0, rows):
            left = pltpu.roll(p, 1, axis=1) * ml
            right = pltpu.roll(p, TT - 1, axis=1) * mr
            rs = slice(r0, r0 + rows)
            halves = []
            for hl in range(0, TT, LANES):
                ls = slice(hl, hl + LANES)
                halves.append(cwb_ref[0, rs, :] * left[:, ls] + cwb_ref[1, rs, :] * p[:, ls]
                              + cwb_ref[2, rs, :] * right[:, ls] + cwb_ref[3, rs, :])
            return jnp.concatenate(halves, axis=1)

        for r in range(0, SSD_W, rc):
            zs_ref[0, r:r + rc, :] = _silu(mm(R_ZS + r, rc)).astype(zs_ref.dtype)
        for r in range(0, XBC_W, rc):
            xbc_ref[0, r:r + rc, :] = _silu(conv(mm(R_XBC + r, rc), r, rc)).astype(xbc_ref.dtype)
        nh = SSD_HEADS
        dt = jax.nn.softplus(mm(R_DT, DT_ROWS)[0:2 * nh, :] + par_ref[:, 0:1])
        a = dt * -jnp.exp(par_ref[:, 1:2])
        pos = lax.broadcasted_iota(jnp.int32, a.shape, 1) & (CHUNK - 1)
        pre = a
        suf = a
        k = 1
        while k < CHUNK:
            pre = pre + jnp.where(pos >= k, pltpu.roll(pre, k, axis=1), 0.0)
            suf = suf + jnp.where(pos < CHUNK - k, pltpu.roll(suf, TT - k, axis=1), 0.0)
            k *= 2
        tot = pre + suf - a
        cum = jnp.concatenate([pre[0:nh], suf[nh:2 * nh]], axis=0)
        da_ref[0] = jnp.concatenate([dt, cum, jnp.exp(tot - cum), jnp.exp(cum), jnp.exp(tot)], axis=0)
        for hc in range(TT // CHUNK):
            sq = jnp.concatenate([cum[:, hc * CHUNK:(hc + 1) * CHUNK],
                                  jnp.zeros((LANES - 2 * nh, CHUNK), F32)], axis=0)
            cc_ref[0, hc * CHUNK:(hc + 1) * CHUNK, :] = sq.T
        for r in range(0, HY_W, rc):
            zh_ref[0, r:r + rc, :] = _silu(mm(R_ZH + r, rc)).astype(zh_ref.dtype)
        for r in range(0, HY_W, rc):
            p0 = conv(mm(R_HY + r, rc), XBC_W + r, rc)
            p1 = conv(mm(R_HY + HY_W + r, rc), XBC_W + HY_W + r, rc)
            pv = conv(mm(R_HY + 2 * HY_W + r, rc), XBC_W + 2 * HY_W + r, rc)
            x0_ref[0, r:r + rc, :] = p0.astype(x0_ref.dtype)
            u_ref[0, r:r + rc, :] = (p1 * pv).astype(u_ref.dtype)

    @pl.when(t >= ALL_TILES)
    def _():
        x0_ref[...] = jnp.zeros_like(x0_ref)
        u_ref[...] = jnp.zeros_like(u_ref)


def _mod_kind(i, t):
    return (i, jnp.where(t >= LAT_TILES, 1, 0), 0, 0)


def _stream_specs(row_major):
    if row_major:
        return [pl.BlockSpec((1, TT, D_MODEL), lambda i, t: (i, jnp.minimum(t, LAT_TILES - 1), 0)),
                pl.BlockSpec((1, CTX_LEN, D_MODEL), lambda i, t: (i, 0, 0))]
    return [pl.BlockSpec((1, D_MODEL, TT), lambda i, t: (i, 0, jnp.minimum(t, ALL_TILES - 1)))]


def _inproj(stream, mod, nw, w_t, layer, cw, par, ml, mr, *, row_major):
    b = stream[0].shape[0]
    vec_specs = [pl.BlockSpec((1, 1) + mod.shape[2:], _mod_kind), pl.BlockSpec(nw.shape, lambda i, t: (0, 0))]
    last = ALL_TILES - 1
    tok = lambda i, t: (i, 0, jnp.minimum(t, last))
    const2 = lambda i, t: (0, 0)
    out_tok = lambda rows: pl.BlockSpec((1, rows, TT), tok)
    return pl.pallas_call(
        functools.partial(_inproj_kernel, row_major=row_major),
        grid=(b, HY_TILES),
        in_specs=_stream_specs(row_major) + vec_specs + [
            pl.BlockSpec((None, R_END, D_MODEL), lambda i, t: (layer, 0, 0), pipeline_mode=pl.Buffered(1)),
            pl.BlockSpec((CONV_ROWS, 4), const2),
            pl.BlockSpec((2 * SSD_HEADS, 2), const2),
            pl.BlockSpec((1, TT), lambda i, t: (0, jnp.minimum(t, last))),
            pl.BlockSpec((1, TT), lambda i, t: (0, jnp.minimum(t, last))),
        ],
        out_specs=[
            out_tok(SSD_W), out_tok(XBC_W), out_tok(DA_ROWS),
            pl.BlockSpec((1, TT, LANES), lambda i, t: (i, jnp.minimum(t, last), 0)),
            out_tok(HY_W),
            pl.BlockSpec((1, HY_W, TT), lambda i, t: (i, 0, t)),
            pl.BlockSpec((1, HY_W, TT), lambda i, t: (i, 0, t)),
        ],
        out_shape=[
            jax.ShapeDtypeStruct((b, SSD_W, T_ALL), ACT),
            jax.ShapeDtypeStruct((b, XBC_W, T_ALL), ACT),
            jax.ShapeDtypeStruct((b, DA_ROWS, T_ALL), F32),
            jax.ShapeDtypeStruct((b, T_ALL, LANES), F32),
            jax.ShapeDtypeStruct((b, HY_W, T_ALL), ACT),
            jax.ShapeDtypeStruct((b, HY_W, T_HY), ACT),
            jax.ShapeDtypeStruct((b, HY_W, T_HY), ACT),
        ],
        scratch_shapes=[pltpu.VMEM((4, CONV_ROWS, LANES), F32)],
        compiler_params=_cp(("parallel", "arbitrary")),
        name="in_projection",
    )(*stream, mod, nw, w_t, cw, par, ml, mr)


def _ssd_direction(d, ck, x_ref, da_ref, cc_ref, dexp_ref, y_ref, st_ref, xdt_ref, w_ref, valid):
    nh = SSD_HEADS
    hd = SSD_HEADDIM
    hpg = nh // SSD_GROUPS
    o = d * nh
    tk = slice(ck * CHUNK, (ck + 1) * CHUNK)
    field = lambda k: da_ref[0, 2 * k * nh + o:2 * k * nh + o + nh, tk]
    dt, cum_t, e_end, e_in, e_tot = (field(k) for k in range(5))
    cum_c = cc_ref[0, tk, o:o + nh]
    for h in range(nh):
        rows = slice(h * hd, (h + 1) * hd)
        xdt = x_ref[0, rows, tk].astype(F32) * dt[h:h + 1, :]
        xdt_ref[rows, :] = xdt.astype(BF)
        w_ref[rows, :] = (xdt * e_end[h:h + 1, :]).astype(BF)
    for g in range(SSD_GROUPS):
        grows = slice(g * hpg * hd, (g + 1) * hpg * hd)
        b_t = x_ref[0, SSD_W + g * SSD_STATE:SSD_W + (g + 1) * SSD_STATE, tk].astype(BF)
        c_t = x_ref[0, SSD_W + (SSD_GROUPS + g) * SSD_STATE:
                    SSD_W + (SSD_GROUPS + g + 1) * SSD_STATE, tk].astype(BF)
        cb_t = lax.dot_general(b_t, c_t, (((0,), (0,)), ((), ())), preferred_element_type=F32)
        inter = jnp.dot(st_ref[grows, :].astype(BF), c_t, preferred_element_type=F32)
        s_new = lax.dot_general(w_ref[grows, :], b_t, (((1,), (1,)), ((), ())),
                                preferred_element_type=F32)
        for h in range(g * hpg, (g + 1) * hpg):
            rows = slice(h * hd, (h + 1) * hd)
            loc = slice((h - g * hpg) * hd, (h - g * hpg + 1) * hd)
            seg = cum_t[h:h + 1, :] - cum_c[:, h:h + 1]
            m_t = (cb_t * jnp.exp(jnp.where(valid, seg, -jnp.inf))).astype(BF)
            intra = jnp.dot(xdt_ref[rows, :], m_t, preferred_element_type=F32)
            y_ref[0, rows, tk] = (intra + inter[loc] * e_in[h:h + 1, :]
                                  + dexp_ref[d, rows, :] * x_ref[0, rows, tk].astype(F32)).astype(y_ref.dtype)
            st_ref[rows, :] = e_tot[h:h + 1, :] * st_ref[rows, :] + s_new[loc]


def _ssd_kernel(xf_ref, xb_ref, daf_ref, dab_ref, ccf_ref, ccb_ref, dexp_ref, yf_ref, yb_ref,
                sf_ref, sb_ref, xdtf_ref, wf_ref, xdtb_ref, wb_ref):
    @pl.when(pl.program_id(1) == 0)
    def _():
        sf_ref[...] = jnp.zeros_like(sf_ref)
        sb_ref[...] = jnp.zeros_like(sb_ref)

    row = lax.broadcasted_iota(jnp.int32, (CHUNK, CHUNK), 0)
    col = lax.broadcasted_iota(jnp.int32, (CHUNK, CHUNK), 1)
    per = SSD_BLOCK // CHUNK
    for k in range(per):
        _ssd_direction(0, k, xf_ref, daf_ref, ccf_ref, dexp_ref, yf_ref, sf_ref, xdtf_ref, wf_ref,
                       row <= col)
        _ssd_direction(1, per - 1 - k, xb_ref, dab_ref, ccb_ref, dexp_ref, yb_ref, sb_ref, xdtb_ref, wb_ref,
                       row >= col)


def _ssd(xbc_t, da_t, cc, dexp):
    b = xbc_t.shape[0]
    nb = T_ALL // SSD_BLOCK
    lat = SEQ // SSD_BLOCK
    fwd = lambda i, s: (i, 0, (s + lat) % nb)
    bwd = lambda i, s: (i, 0, nb - 1 - s)
    fwd_r = lambda i, s: (i, (s + lat) % nb, 0)
    bwd_r = lambda i, s: (i, nb - 1 - s, 0)
    return pl.pallas_call(
        _ssd_kernel,
        grid=(b, nb),
        in_specs=[
            pl.BlockSpec((1, XBC_W, SSD_BLOCK), fwd),
            pl.BlockSpec((1, XBC_W, SSD_BLOCK), bwd),
            pl.BlockSpec((1, DA_ROWS, SSD_BLOCK), fwd),
            pl.BlockSpec((1, DA_ROWS, SSD_BLOCK), bwd),
            pl.BlockSpec((1, SSD_BLOCK, LANES), fwd_r),
            pl.BlockSpec((1, SSD_BLOCK, LANES), bwd_r),
            pl.BlockSpec((2, SSD_W, CHUNK), lambda i, s: (0, 0, 0)),
        ],
        out_specs=[pl.BlockSpec((1, SSD_W, SSD_BLOCK), fwd), pl.BlockSpec((1, SSD_W, SSD_BLOCK), bwd)],
        out_shape=[jax.ShapeDtypeStruct((b, SSD_W, T_ALL), ACT)] * 2,
        scratch_shapes=[pltpu.VMEM((SSD_W, SSD_STATE), F32)] * 2 + [pltpu.VMEM((SSD_W, CHUNK), BF)] * 4,
        compiler_params=_cp(("parallel", "arbitrary")),
        name="ssd_scan",
    )(xbc_t, xbc_t, da_t, da_t, cc, cc, dexp)


def _split_bf16(v):
    hi = v.astype(BF)
    return hi, (v - hi.astype(F32)).astype(BF)


def _dot3(a, b):
    a_hi, a_lo = _split_bf16(a)
    b_hi, b_lo = _split_bf16(b)
    return jnp.dot(jnp.concatenate([a_hi, a_hi, a_lo], axis=1),
                   jnp.concatenate([b_hi, b_lo, b_hi], axis=0), preferred_element_type=F32)


def _filt_hidden_kernel(f_ref, w1_ref, w2_ref, w3_ref, bf_ref, o_ref):
    fr = bf_ref[:, 3:4]
    hcur = jnp.sin(fr * (_dot3(w1_ref[...], f_ref[...]) + bf_ref[:, 0:1]))
    hcur = jnp.sin(fr * (_dot3(w2_ref[...], hcur) + bf_ref[:, 1:2]))
    o_ref[...] = jnp.sin(fr * (_dot3(w3_ref[...], hcur) + bf_ref[:, 2:3]))


def _filt_kernel(h_ref, w4_ref, pos_ref, dl_ref, o_ref, *, both):
    hid = h_ref[...]
    window = jnp.exp(-dl_ref[...] * pos_ref[0:1, :])
    if both:
        filt = _dot3(w4_ref[0], hid) * pos_ref[1:2, :] + _dot3(w4_ref[1], hid) * pos_ref[2:3, :]
    else:
        filt = _dot3(w4_ref[0], hid) * (pos_ref[1:2, :] + pos_ref[2:3, :])
    o_ref[...] = (filt * window).reshape(o_ref.shape)


def _hyena_filter(feats, pos, length, w1t, w2t, w3t, bfr, w4t, deltas):
    n = feats.shape[1]
    nt = min(n, 2048)
    full = lambda shape: pl.BlockSpec(shape, lambda j: tuple(0 for _ in shape))
    hid = pl.pallas_call(
        _filt_hidden_kernel,
        grid=(n // nt,),
        in_specs=[pl.BlockSpec((HY_FILT_W, nt), lambda j: (0, j)),
                  full((HY_FILT_W, HY_FILT_W)), full((HY_FILT_W, HY_FILT_W)),
                  full((HY_FILT_W, HY_FILT_W)), full((HY_FILT_W, 4))],
        out_specs=pl.BlockSpec((HY_FILT_W, nt), lambda j: (0, j)),
        out_shape=jax.ShapeDtypeStruct((HY_FILT_W, n), F32),
        compiler_params=_cp(("arbitrary",)),
        name="hyena_filter_hidden",
    )(feats, w1t, w2t, w3t, bfr)
    ct = 256
    tiles = n // nt
    both = tiles == 1
    assert both or n == 2 * length
    w4_spec = (pl.BlockSpec((2, ct, HY_FILT_W), lambda c, j: (0, c, 0)) if both else
               pl.BlockSpec((1, ct, HY_FILT_W), lambda c, j: (j // (tiles // 2), c, 0)))
    return pl.pallas_call(
        functools.partial(_filt_kernel, both=both),
        grid=(HY_W // ct, tiles),
        in_specs=[pl.BlockSpec((HY_FILT_W, nt), lambda c, j: (0, j)),
                  w4_spec,
                  pl.BlockSpec((8, nt), lambda c, j: (0, j)),
                  pl.BlockSpec((ct, 1), lambda c, j: (c, 0))],
        out_specs=pl.BlockSpec((ct, nt // LANES, LANES), lambda c, j: (c, j, 0)),
        out_shape=jax.ShapeDtypeStruct((HY_W, n // LANES, LANES), F32),
        compiler_params=_cp(("parallel", "arbitrary")),
        name="hyena_filter",
    )(hid, w4t, pos, deltas)


def _hyconv_kernel(u_ref, x0_ref, f_ref, skip_ref, g1_ref, g1f_ref, g1i_ref, tw_ref, w3_ref, w3c_ref,
                   o_ref, ub_ref, l3_ref, l3f_ref, hs_ref, bp_ref, *, n1, r, ct):
    twr = tw_ref[0]
    twi = tw_ref[1]
    for bb in range(2):
        ub_ref[bb] = u_ref[bb].astype(F32).reshape(ct, r, LANES)

    def twiddled_rows(dst_ref, col, c):
        cr, ci = col[:n1], col[n1:]
        dst_ref[c * n1:(c + 1) * n1, 0:LANES] = (cr * twr - ci * twi).astype(BF)
        dst_ref[c * n1:(c + 1) * n1, LANES:2 * LANES] = (cr * twi + ci * twr).astype(BF)

    def column_dfts(c):
        rhs = jnp.concatenate(
            [jnp.concatenate([ub_ref[0, c + j], ub_ref[1, c + j]], axis=0) for j in range(2)],
            axis=1).astype(BF)
        bcol = jnp.dot(g1_ref[...], rhs, preferred_element_type=F32)
        for j in range(2):
            twiddled_rows(l3_ref, bcol[:, j * LANES:(j + 1) * LANES], c + j)

    def filter_column_dfts(c):
        frhs = jnp.concatenate([f_ref[c], f_ref[c + 1]], axis=1).astype(BF)
        fcol = jnp.dot(g1f_ref[...], frhs, preferred_element_type=F32)
        for j in range(2):
            twiddled_rows(l3f_ref, fcol[:, j * LANES:(j + 1) * LANES], c + j)

    rows_b = 256

    def filter_spectrum(k):
        rows = slice(k * rows_b, (k + 1) * rows_b)
        hs_ref[rows, :] = jnp.dot(l3f_ref[rows, :], w3_ref[...], preferred_element_type=F32)

    def spectrum_product(k):
        rows = slice(k * rows_b, (k + 1) * rows_b)
        xs = jnp.dot(l3_ref[rows, :], w3_ref[...], preferred_element_type=F32)
        xr, xi = xs[:, :LANES], xs[:, LANES:]
        hr, hi = hs_ref[rows, 0:LANES], hs_ref[rows, LANES:2 * LANES]
        ys = jnp.concatenate([xr * hr - xi * hi, xr * hi + xi * hr], axis=1).astype(BF)
        bp_ref[rows, :] = jnp.dot(ys, w3c_ref[...], preferred_element_type=F32)

    def untwiddled(c):
        bp = bp_ref[c * n1:(c + 1) * n1, :]
        br, bi = bp[:, :LANES], bp[:, LANES:]
        return jnp.concatenate([br * twr + bi * twi, bi * twr - br * twi], axis=0)

    def inverse_column_dfts(c):
        rhs = jnp.concatenate([untwiddled(c), untwiddled(c + 1)], axis=1).astype(BF)
        out = jnp.dot(g1i_ref[...], rhs, preferred_element_type=F32)
        for j in range(2):
            ub_ref[0, c + j] = out[:r, j * LANES:(j + 1) * LANES]
            ub_ref[1, c + j] = out[r:, j * LANES:(j + 1) * LANES]

    for c in range(0, ct, 2):
        filter_column_dfts(c)
    for k in range(ct * n1 // rows_b):
        filter_spectrum(k)
    for c in range(0, ct, 2):
        column_dfts(c)
    for k in range(ct * n1 // rows_b):
        spectrum_product(k)
    for c in range(0, ct, 2):
        inverse_column_dfts(c)

    skip = skip_ref[...]
    for bb in range(2):
        conv = ub_ref[bb].reshape(ct, r * LANES)
        o_ref[bb] = (x0_ref[bb].astype(F32) * (conv + skip * u_ref[bb].astype(F32))).astype(o_ref.dtype)


@functools.lru_cache(maxsize=None)
def _dft_constants(n1, r):
    n = n1 * LANES
    k1 = np.arange(n1)
    f1 = np.exp(-2j * np.pi * np.outer(k1, k1) / n1)
    f2 = np.exp(-2j * np.pi * np.outer(np.arange(LANES), np.arange(LANES)) / LANES)
    tw = np.exp(-2j * np.pi * np.outer(k1, np.arange(LANES)) / n)
    g1 = np.block([[f1.real[:, :r], -f1.imag[:, :r]], [f1.imag[:, :r], f1.real[:, :r]]])
    g1f = np.concatenate([f1.real, f1.imag], axis=0)
    g1i = np.block([[f1.real[:r], f1.imag[:r]], [-f1.imag[:r], f1.real[:r]]]) / n
    w3 = np.block([[f2.real, f2.imag], [-f2.imag, f2.real]])
    w3c = np.block([[f2.real, -f2.imag], [f2.imag, f2.real]])
    tws = np.stack([tw.real, tw.imag])
    return tuple(np.asarray(a, np.float32) for a in (g1, g1f, g1i, tws, w3, w3c))


def _hyconv(u_t, x0_t, filt, skip, prev, *, n1, r, tok_block):
    b, c_all = u_t.shape[0], u_t.shape[1]
    ct = 16 if n1 >= 128 else 64
    g1, g1f, g1i, tws, w3, w3c = (jnp.asarray(a) for a in _dft_constants(n1, r))
    g1, g1f, g1i, w3, w3c = (a.astype(BF) for a in (g1, g1f, g1i, w3, w3c))
    sig = pl.BlockSpec((b, ct, r * LANES), lambda i: (0, i, tok_block))
    full = lambda a: pl.BlockSpec(a.shape, lambda i: tuple(0 for _ in a.shape))
    args = [u_t, x0_t, filt, skip, g1, g1f, g1i, tws, w3, w3c]
    in_specs = [sig, sig, pl.BlockSpec((ct, n1, LANES), lambda i: (i, 0, 0)),
                pl.BlockSpec((ct, 1), lambda i: (i, 0)),
                full(g1), full(g1f), full(g1i), full(tws), full(w3), full(w3c)]
    aliases = {}
    kern = functools.partial(_hyconv_kernel, n1=n1, r=r, ct=ct)
    if prev is not None:
        args.append(prev)
        in_specs.append(pl.BlockSpec(memory_space=pl.ANY))
        aliases = {len(args) - 1: 0}
        body = kern
        kern = lambda *refs: body(*refs[:10], *refs[11:])
    return pl.pallas_call(
        kern,
        grid=(c_all // ct,),
        in_specs=in_specs,
        out_specs=sig,
        out_shape=jax.ShapeDtypeStruct(u_t.shape, ACT),
        scratch_shapes=[pltpu.VMEM((b, ct, r, LANES), F32),
                        pltpu.VMEM((ct * n1, 2 * LANES), BF), pltpu.VMEM((ct * n1, 2 * LANES), BF),
                        pltpu.VMEM((ct * n1, 2 * LANES), F32), pltpu.VMEM((ct * n1, 2 * LANES), F32)],
        input_output_aliases=aliases,
        compiler_params=_cp(("arbitrary",)),
        name="hyena_conv_n%d" % n1,
    )(*args)


def _merge_kernel(*refs, final, row_major):
    yf_ref, yb_ref, zs_ref, yh_ref, zh_ref, *refs = refs
    if row_major:
        x_ref, ctx_ref, mod_ref, gw_ref, w_ref, fw_ref, o_ref = refs
        x = jnp.where(pl.program_id(1) < LAT_TILES, x_ref[0], ctx_ref[0]).T
    else:
        x_ref, mod_ref, gw_ref, w_ref, fw_ref, o_ref = refs
        x = x_ref[0]
    gate = mod_ref[0, 0, :, 2:3]

    def projected_group(g, cols):
        rs = lax.rsqrt(jnp.mean(g * g, axis=0, keepdims=True) + EPS)
        scaled = (g * gw_ref[cols, :]).astype(BF)
        return jnp.dot(w_ref[:, cols], scaled, preferred_element_type=F32) * rs

    g_s = (yf_ref[0].astype(F32) + yb_ref[0].astype(F32)) * zs_ref[0].astype(F32)
    g_h = yh_ref[0].astype(F32) * zh_ref[0].astype(F32)
    out = projected_group(g_s, slice(0, SSD_W)) + projected_group(g_h, slice(SSD_W, SSD_W + HY_W))
    xn = x + gate * out
    if final:
        ms = jnp.mean(xn * xn, axis=0, keepdims=True)
        o_ref[0] = ((xn * lax.rsqrt(ms + EPS)) * fw_ref[...]).T
    else:
        o_ref[0] = xn


def _merge(yf, yb, zs, yh, zh, stream, mod, gw, w_out_t, layer, fw, *, final, row_major):
    b = yf.shape[0]
    tiles = LAT_TILES if final else ALL_TILES
    tok = lambda i, t: (i, 0, t)
    const2 = lambda i, t: (0, 0)
    act = lambda rows: pl.BlockSpec((1, rows, TT), tok)
    if final:
        out_spec = pl.BlockSpec((1, TT, D_MODEL), lambda i, t: (i, t, 0))
        out_shape = jax.ShapeDtypeStruct((b, SEQ, D_MODEL), F32)
    else:
        out_spec = act(D_MODEL)
        out_shape = jax.ShapeDtypeStruct((b, D_MODEL, T_ALL), F32)
    return pl.pallas_call(
        functools.partial(_merge_kernel, final=final, row_major=row_major),
        grid=(b, tiles),
        in_specs=[act(SSD_W), act(SSD_W), act(SSD_W), act(HY_W), act(HY_W)] + _stream_specs(row_major) + [
                  pl.BlockSpec((1, 1, D_MODEL, 4), _mod_kind),
                  pl.BlockSpec((SSD_W + HY_W, 1), const2),
                  pl.BlockSpec((None, D_MODEL, SSD_W + HY_W), lambda i, t: (layer, 0, 0)),
                  pl.BlockSpec((D_MODEL, 1), const2)],
        out_specs=out_spec,
        out_shape=out_shape,
        compiler_params=_cp(("parallel", "arbitrary")),
        name="merge_out_projection",
    )(yf, yb, zs, yh, zh, *stream, mod, gw, w_out_t, fw)


@functools.lru_cache(maxsize=None)
def _conv_masks():
    t = np.arange(T_ALL)
    pos = np.where(t < SEQ, t % GRID_W, t - SEQ)
    period = np.where(t < SEQ, GRID_W, CTX_LEN)
    ml = (pos != 0).astype(np.float32)[None, :]
    mr = (pos != period - 1).astype(np.float32)[None, :]
    return ml, mr


@functools.lru_cache(maxsize=None)
def _filter_positions(n, length):
    idx = np.arange(n)
    fwd = idx < length
    bwd = idx > n - length
    d = np.where(fwd, idx, np.where(bwd, n - idx, 0)).astype(np.float64)
    t = (np.linspace(0.0, 1.0, length, dtype=np.float32).astype(np.float64))[d.astype(np.int64)]
    w = (2.0 * math.pi / length) * d
    f = np.linspace(1e-4, HY_BANDS - 1, HY_BANDS, dtype=np.float32).astype(np.float64)[:, None]
    feats = np.zeros((HY_FILT_W, n), np.float32)
    feats[0] = t
    feats[1:1 + HY_BANDS] = np.cos(f * w[None, :])
    feats[1 + HY_BANDS:HY_EMB] = -np.sin(f * w[None, :])
    pos = np.zeros((8, n), np.float32)
    pos[0] = t
    pos[1] = fwd
    pos[2] = bwd
    return feats, pos


def kernel(x, c, ctx, c_ctx, norm_w, ada_w, ada_b, w_in, ssd_conv_w, ssd_conv_b, dt_bias, a_log, d_skip,
           hy_conv_w, hy_conv_b, filt_w1, filt_b1, filt_w2, filt_b2, filt_w3, filt_b3, filt_w4, filt_freq,
           hy_bias, gnorm_w, w_out, final_norm_w):
    bsz = x.shape[0]
    depth = norm_w.shape[0]
    assert x.shape == (bsz, SEQ, D_MODEL) and ctx.shape == (bsz, CTX_LEN, D_MODEL) and bsz == 2

    s_rows = jnp.zeros((8, D_MODEL), F32).at[:bsz].set(c).at[bsz].set(c_ctx)
    mods = _ada(s_rows, ada_w, ada_b)
    mods = mods[:, :bsz + 1].reshape(depth, bsz + 1, 3, D_MODEL)
    lat = mods[:, :bsz]
    cx = jnp.broadcast_to(mods[:, bsz:bsz + 1], lat.shape)
    mod = jnp.stack([lat, cx], axis=2)
    mod = jnp.stack([mod[:, :, :, 0], 1.0 + mod[:, :, :, 1], mod[:, :, :, 2], jnp.zeros_like(mod[:, :, :, 0])],
                    axis=-1)
    mod_rows = jnp.swapaxes(mod, -1, -2)

    ml, mr = (jnp.asarray(a) for a in _conv_masks())
    deltas = jnp.abs(jnp.linspace(HY_MIN_DECAY, HY_MAX_DECAY, HY_W, dtype=F32))[:, None]

    w_in_t = jnp.swapaxes(w_in, 1, 2).astype(BF)
    w_out_t = jnp.swapaxes(w_out, 1, 2).astype(BF)

    stream = (x, ctx)
    out = None
    for l in range(depth):
        last = l == depth - 1
        row_major = l == 0
        cw = jnp.concatenate([
            jnp.concatenate([ssd_conv_w[l].T, ssd_conv_b[l][:, None]], axis=1),
            jnp.concatenate([hy_conv_w[l].T, hy_conv_b[l][:, None]], axis=1)], axis=0)
        par = jnp.stack([dt_bias[l].reshape(-1), a_log[l].reshape(-1)], axis=1)
        if row_major:
            vecs = (mod_rows[l], norm_w[l][None, :])
        else:
            vecs = (mod[l], norm_w[l][:, None])
        zs, xbc, da, cc, zh, x0, u = _inproj(stream, *vecs, w_in_t, l, cw, par, ml, mr, row_major=row_major)

        dexp = jnp.broadcast_to(jnp.repeat(d_skip[l], SSD_HEADDIM, axis=1)[:, :, None],
                                (2, SSD_W, CHUNK))
        yf, yb = _ssd(xbc, da, cc, dexp)

        w4t = filt_w4[l].T.reshape(2, HY_W, HY_FILT_W)
        bfr = jnp.stack([filt_b1[l], filt_b2[l], filt_b3[l], filt_freq[l]], axis=1)
        w1t = jnp.pad(filt_w1[l].T, ((0, 0), (0, HY_FILT_W - HY_EMB)))
        skip = hy_bias[l][:, None]

        def long_conv(n1, r, tok_block, length, prev):
            feats, pos = (jnp.asarray(a) for a in _filter_positions(n1 * LANES, length))
            filt = _hyena_filter(feats, pos, length, w1t, filt_w2[l].T, filt_w3[l].T, bfr, w4t, deltas)
            return _hyconv(u, x0, filt, skip, prev, n1=n1, r=r, tok_block=tok_block)

        yh = long_conv(2 * SEQ // LANES, SEQ // LANES, 0, SEQ, None)
        if not last:
            yh = long_conv(16, HY_CTX_ROWS, SEQ // (HY_CTX_ROWS * LANES), CTX_LEN, yh)

        res = _merge(yf, yb, zs, yh, zh, stream, mod[l], gnorm_w[l][:, None], w_out_t, l,
                     final_norm_w[:, None], final=last, row_major=row_major)
        if last:
            out = res
        else:
            stream = (res,)
    return out
```

```python
import functools
import math

import numpy as np
import jax
import jax.numpy as jnp
from jax import lax
from jax.experimental import pallas as pl
from jax.experimental.pallas import tpu as pltpu

D_MODEL = 1024
SEQ = 8192
CTX_LEN = 256
GRID_W = 64
SSD_W = 1024
HY_W = 1024
SSD_HEADDIM = 64
SSD_HEADS = SSD_W // SSD_HEADDIM
SSD_GROUPS = 2
SSD_STATE = 128
CHUNK = 128
HY_EMB = 33
HY_BANDS = (HY_EMB - 1) // 2
HY_FILT_W = 64
HY_TARGET = 1e-2
HY_MIN_DECAY = math.log(HY_TARGET) / 1.5
HY_MAX_DECAY = math.log(HY_TARGET) / 0.3
EPS = 1e-6
XBC_W = SSD_W + 2 * SSD_GROUPS * SSD_STATE
COL_ZS = 0
COL_XBC = COL_ZS + SSD_W
COL_DT = COL_XBC + XBC_W
COL_ZH = COL_DT + 2 * SSD_HEADS
COL_HY = COL_ZH + HY_W

LANES = 128
T_ALL = SEQ + CTX_LEN
TT = 256
LAT_TILES = SEQ // TT
ALL_TILES = T_ALL // TT
HY_CTX_ROWS = 8
T_HY = SEQ + HY_CTX_ROWS * LANES
HY_TILES = T_HY // TT
DT_ROWS = 128
DA_ROWS = 5 * 2 * SSD_HEADS
SSD_BLOCK = 2 * CHUNK
R_ZS = COL_ZS
R_XBC = COL_XBC
R_DT = COL_DT
R_ZH = COL_ZH
R_HY = COL_HY
R_END = COL_HY + 3 * HY_W
CONV_ROWS = XBC_W + 3 * HY_W

ACT = jnp.bfloat16
BF = jnp.bfloat16
F32 = jnp.float32
HI = lax.Precision.HIGHEST
VMEM_LIMIT = 56 * 1024 * 1024


def _cp(sem, flags=None):
    return pltpu.CompilerParams(dimension_semantics=sem, vmem_limit_bytes=VMEM_LIMIT, flags=flags)


def _silu(v):
    return v * jax.nn.sigmoid(v)


def _ada_kernel(s_ref, w_ref, b_ref, o_ref):
    s = _silu(s_ref[...])
    o_ref[0] = jnp.dot(s, w_ref[0], precision=HI, preferred_element_type=F32) + b_ref[0]


def _ada(s_rows, ada_w, ada_b):
    depth, d, n3 = ada_w.shape
    ct = 1024
    return pl.pallas_call(
        _ada_kernel,
        grid=(depth, n3 // ct),
        in_specs=[
            pl.BlockSpec((8, d), lambda l, j: (0, 0)),
            pl.BlockSpec((1, d, ct), lambda l, j: (l, 0, j)),
            pl.BlockSpec((1, 1, ct), lambda l, j: (l, 0, j)),
        ],
        out_specs=pl.BlockSpec((1, 8, ct), lambda l, j: (l, 0, j)),
        out_shape=jax.ShapeDtypeStruct((depth, 8, n3), F32),
        compiler_params=_cp(("arbitrary", "arbitrary")),
        name="ada_modulation",
    )(s_rows, ada_w, ada_b.reshape(depth, 1, n3))


def _inproj_kernel(*refs, row_major):
    if row_major:
        x_ref, ctx_ref, *refs = refs
    else:
        x_ref, *refs = refs
    (mod_ref, nw_ref, w_ref, cw_ref, par_ref, ml_ref, mr_ref,
     zs_ref, xbc_ref, da_ref, cc_ref, g0_ref, u_ref, cwb_ref) = refs
    t = pl.program_id(1)
    rc = 256

    @pl.when(t == 0)
    def _():
        for r in range(0, CONV_ROWS, 512):
            for k in range(4):
                cwb_ref[k, r:r + 512, :] = jnp.broadcast_to(cw_ref[r:r + 512, k:k + 1], (512, LANES))

    @pl.when(t < ALL_TILES)
    def _():
        if row_major:
            x = jnp.where(t < LAT_TILES, x_ref[0], ctx_ref[0])
            ms = jnp.mean(x * x, axis=1, keepdims=True)
            xn = x * lax.rsqrt(ms + EPS)
            xb = ((xn * nw_ref[...]) * mod_ref[0, 0, 1:2, :] + mod_ref[0, 0, 0:1, :]).astype(BF)
            dims = (((1,), (1,)), ((), ()))
        else:
            x = x_ref[0]
            ms = jnp.mean(x * x, axis=0, keepdims=True)
            xn = x * lax.rsqrt(ms + EPS)
            xb = ((xn * nw_ref[...]) * mod_ref[0, 0, :, 1:2] + mod_ref[0, 0, :, 0:1]).astype(BF)
            dims = (((1,), (0,)), ((), ()))
        ml = ml_ref[...]
        mr = mr_ref[...]

        def mm(r0, rows):
            return lax.dot_general(w_ref[r0:r0 + rows, :], xb, dims, preferred_element_type=F32)

        def conv(p, r0, rows):
            left = pltpu.roll(p, 1, axis=1) * ml
            right = pltpu.roll(p, TT - 1, axis=1) * mr
            rs = slice(r0, r0 + rows)
            halves = []
            for hl in range(0, TT, LANES):
                ls = slice(hl, hl + LANES)
                halves.append(cwb_ref[0, rs, :] * left[:, ls] + cwb_ref[1, rs, :] * p[:, ls]
                              + cwb_ref[2, rs, :] * right[:, ls] + cwb_ref[3, rs, :])
            return jnp.concatenate(halves, axis=1)

        for r in range(0, SSD_W, rc):
            zs_ref[0, r:r + rc, :] = _silu(mm(R_ZS + r, rc)).astype(zs_ref.dtype)
        for r in range(0, XBC_W, rc):
            xbc_ref[0, r:r + rc, :] = _silu(conv(mm(R_XBC + r, rc), r, rc)).astype(xbc_ref.dtype)
        nh = SSD_HEADS
        dt = jax.nn.softplus(mm(R_DT, DT_ROWS)[0:2 * nh, :] + par_ref[:, 0:1])
        a = dt * -jnp.exp(par_ref[:, 1:2])
        pos = lax.broadcasted_iota(jnp.int32, a.shape, 1) & (CHUNK - 1)
        pre = a
        suf = a
        k = 1
        while k < CHUNK:
            pre = pre + jnp.where(pos >= k, pltpu.roll(pre, k, axis=1), 0.0)
            suf = suf + jnp.where(pos < CHUNK - k, pltpu.roll(suf, TT - k, axis=1), 0.0)
            k *= 2
        tot = pre + suf - a
        cum = jnp.concatenate([pre[0:nh], suf[nh:2 * nh]], axis=0)
        da_ref[0] = jnp.concatenate([dt, cum, jnp.exp(tot - cum), jnp.exp(cum), jnp.exp(tot)], axis=0)
        for hc in range(TT // CHUNK):
            sq = jnp.concatenate([cum[:, hc * CHUNK:(hc + 1) * CHUNK],
                                  jnp.zeros((LANES - 2 * nh, CHUNK), F32)], axis=0)
            cc_ref[0, hc * CHUNK:(hc + 1) * CHUNK, :] = sq.T
        for r in range(0, HY_W, rc):
            gate = _silu(mm(R_ZH + r, rc)) * conv(mm(R_HY + r, rc), XBC_W + r, rc)
            g0_ref[0, r:r + rc, :] = gate.astype(g0_ref.dtype)
            p1 = conv(mm(R_HY + HY_W + r, rc), XBC_W + HY_W + r, rc)
            pv = conv(mm(R_HY + 2 * HY_W + r, rc), XBC_W + 2 * HY_W + r, rc)
            u_ref[0, r:r + rc, :] = (p1 * pv).astype(u_ref.dtype)

    @pl.when(t >= ALL_TILES)
    def _():
        u_ref[...] = jnp.zeros_like(u_ref)


def _mod_kind(i, t):
    return (i, jnp.where(t >= LAT_TILES, 1, 0), 0, 0)


def _stream_specs(row_major):
    if row_major:
        return [pl.BlockSpec((1, TT, D_MODEL), lambda i, t: (i, jnp.minimum(t, LAT_TILES - 1), 0)),
                pl.BlockSpec((1, CTX_LEN, D_MODEL), lambda i, t: (i, 0, 0))]
    return [pl.BlockSpec((1, D_MODEL, TT), lambda i, t: (i, 0, jnp.minimum(t, ALL_TILES - 1)))]


def _inproj(stream, mod, nw, w_t, layer, cw, par, ml, mr, *, row_major):
    b = stream[0].shape[0]
    vec_specs = [pl.BlockSpec((1, 1) + mod.shape[2:], _mod_kind), pl.BlockSpec(nw.shape, lambda i, t: (0, 0))]
    last = ALL_TILES - 1
    tok = lambda i, t: (i, 0, jnp.minimum(t, last))
    const2 = lambda i, t: (0, 0)
    out_tok = lambda rows: pl.BlockSpec((1, rows, TT), tok)
    return pl.pallas_call(
        functools.partial(_inproj_kernel, row_major=row_major),
        grid=(b, HY_TILES),
        in_specs=_stream_specs(row_major) + vec_specs + [
            pl.BlockSpec((None, R_END, D_MODEL), lambda i, t: (layer, 0, 0), pipeline_mode=pl.Buffered(1)),
            pl.BlockSpec((CONV_ROWS, 4), const2),
            pl.BlockSpec((2 * SSD_HEADS, 2), const2),
            pl.BlockSpec((1, TT), lambda i, t: (0, jnp.minimum(t, last))),
            pl.BlockSpec((1, TT), lambda i, t: (0, jnp.minimum(t, last))),
        ],
        out_specs=[
            out_tok(SSD_W), out_tok(XBC_W), out_tok(DA_ROWS),
            pl.BlockSpec((1, TT, LANES), lambda i, t: (i, jnp.minimum(t, last), 0)),
            out_tok(HY_W),
            pl.BlockSpec((1, HY_W, TT), lambda i, t: (i, 0, t)),
        ],
        out_shape=[
            jax.ShapeDtypeStruct((b, SSD_W, T_ALL), ACT),
            jax.ShapeDtypeStruct((b, XBC_W, T_ALL), ACT),
            jax.ShapeDtypeStruct((b, DA_ROWS, T_ALL), F32),
            jax.ShapeDtypeStruct((b, T_ALL, LANES), F32),
            jax.ShapeDtypeStruct((b, HY_W, T_ALL), ACT),
            jax.ShapeDtypeStruct((b, HY_W, T_HY), ACT),
        ],
        scratch_shapes=[pltpu.VMEM((4, CONV_ROWS, LANES), F32)],
        compiler_params=_cp(("parallel", "arbitrary")),
        name="in_projection",
    )(*stream, mod, nw, w_t, cw, par, ml, mr)


def _ssd_direction(d, ck, x_ref, da_ref, cc_ref, dexp_ref, y_ref, st_ref, xdt_ref, w_ref, valid):
    nh = SSD_HEADS
    hd = SSD_HEADDIM
    hpg = nh // SSD_GROUPS
    o = d * nh
    tk = slice(ck * CHUNK, (ck + 1) * CHUNK)
    field = lambda k: da_ref[0, 2 * k * nh + o:2 * k * nh + o + nh, tk]
    dt, cum_t, e_end, e_in, e_tot = (field(k) for k in range(5))
    cum_c = cc_ref[0, tk, o:o + nh]
    for h in range(nh):
        rows = slice(h * hd, (h + 1) * hd)
        xdt = x_ref[0, rows, tk].astype(F32) * dt[h:h + 1, :]
        xdt_ref[rows, :] = xdt.astype(BF)
        w_ref[rows, :] = (xdt * e_end[h:h + 1, :]).astype(BF)
    for g in range(SSD_GROUPS):
        grows = slice(g * hpg * hd, (g + 1) * hpg * hd)
        b_t = x_ref[0, SSD_W + g * SSD_STATE:SSD_W + (g + 1) * SSD_STATE, tk].astype(BF)
        c_t = x_ref[0, SSD_W + (SSD_GROUPS + g) * SSD_STATE:
                    SSD_W + (SSD_GROUPS + g + 1) * SSD_STATE, tk].astype(BF)
        cb_t = lax.dot_general(b_t, c_t, (((0,), (0,)), ((), ())), preferred_element_type=F32)
        inter = jnp.dot(st_ref[grows, :].astype(BF), c_t, preferred_element_type=F32)
        s_new = lax.dot_general(w_ref[grows, :], b_t, (((1,), (1,)), ((), ())),
                                preferred_element_type=F32)
        for h in range(g * hpg, (g + 1) * hpg):
            rows = slice(h * hd, (h + 1) * hd)
            loc = slice((h - g * hpg) * hd, (h - g * hpg + 1) * hd)
            seg = cum_t[h:h + 1, :] - cum_c[:, h:h + 1]
            m_t = (cb_t * jnp.exp(jnp.where(valid, seg, -jnp.inf))).astype(BF)
            intra = jnp.dot(xdt_ref[rows, :], m_t, preferred_element_type=F32)
            y_ref[0, rows, tk] = (intra + inter[loc] * e_in[h:h + 1, :]
                                  + dexp_ref[d, rows, :] * x_ref[0, rows, tk].astype(F32)).astype(y_ref.dtype)
            st_ref[rows, :] = e_tot[h:h + 1, :] * st_ref[rows, :] + s_new[loc]


def _ssd_kernel(xf_ref, xb_ref, daf_ref, dab_ref, ccf_ref, ccb_ref, dexp_ref, yf_ref, yb_ref,
                sf_ref, sb_ref, xdtf_ref, wf_ref, xdtb_ref, wb_ref):
    @pl.when(pl.program_id(1) == 0)
    def _():
        sf_ref[...] = jnp.zeros_like(sf_ref)
        sb_ref[...] = jnp.zeros_like(sb_ref)

    row = lax.broadcasted_iota(jnp.int32, (CHUNK, CHUNK), 0)
    col = lax.broadcasted_iota(jnp.int32, (CHUNK, CHUNK), 1)
    per = SSD_BLOCK // CHUNK
    for k in range(per):
        _ssd_direction(0, k, xf_ref, daf_ref, ccf_ref, dexp_ref, yf_ref, sf_ref, xdtf_ref, wf_ref,
                       row <= col)
        _ssd_direction(1, per - 1 - k, xb_ref, dab_ref, ccb_ref, dexp_ref, yb_ref, sb_ref, xdtb_ref, wb_ref,
                       row >= col)


def _ssd(xbc_t, da_t, cc, dexp):
    b = xbc_t.shape[0]
    nb = T_ALL // SSD_BLOCK
    lat = SEQ // SSD_BLOCK
    fwd = lambda i, s: (i, 0, (s + lat) % nb)
    bwd = lambda i, s: (i, 0, nb - 1 - s)
    fwd_r = lambda i, s: (i, (s + lat) % nb, 0)
    bwd_r = lambda i, s: (i, nb - 1 - s, 0)
    return pl.pallas_call(
        _ssd_kernel,
        grid=(b, nb),
        in_specs=[
            pl.BlockSpec((1, XBC_W, SSD_BLOCK), fwd),
            pl.BlockSpec((1, XBC_W, SSD_BLOCK), bwd),
            pl.BlockSpec((1, DA_ROWS, SSD_BLOCK), fwd),
            pl.BlockSpec((1, DA_ROWS, SSD_BLOCK), bwd),
            pl.BlockSpec((1, SSD_BLOCK, LANES), fwd_r),
            pl.BlockSpec((1, SSD_BLOCK, LANES), bwd_r),
            pl.BlockSpec((2, SSD_W, CHUNK), lambda i, s: (0, 0, 0)),
        ],
        out_specs=[pl.BlockSpec((1, SSD_W, SSD_BLOCK), fwd), pl.BlockSpec((1, SSD_W, SSD_BLOCK), bwd)],
        out_shape=[jax.ShapeDtypeStruct((b, SSD_W, T_ALL), ACT)] * 2,
        scratch_shapes=[pltpu.VMEM((SSD_W, SSD_STATE), F32)] * 2 + [pltpu.VMEM((SSD_W, CHUNK), BF)] * 4,
        compiler_params=_cp(("parallel", "arbitrary")),
        name="ssd_scan",
    )(xbc_t, xbc_t, da_t, da_t, cc, cc, dexp)


def _split_bf16(v):
    hi = v.astype(BF)
    return hi, (v - hi.astype(F32)).astype(BF)


def _dot3(a, b):
    a_hi, a_lo = _split_bf16(a)
    b_hi, b_lo = _split_bf16(b)
    return jnp.dot(jnp.concatenate([a_hi, a_hi, a_lo], axis=1),
                   jnp.concatenate([b_hi, b_lo, b_hi], axis=0), preferred_element_type=F32)


def _filt_hidden_kernel(f_ref, w1_ref, w2_ref, w3_ref, bf_ref, o_ref):
    fr = bf_ref[:, 3:4]
    hcur = jnp.sin(fr * (_dot3(w1_ref[...], f_ref[...]) + bf_ref[:, 0:1]))
    hcur = jnp.sin(fr * (_dot3(w2_ref[...], hcur) + bf_ref[:, 1:2]))
    o_ref[...] = jnp.sin(fr * (_dot3(w3_ref[...], hcur) + bf_ref[:, 2:3]))


def _filt_kernel(h_ref, w4_ref, pos_ref, dl_ref, o_ref, *, both):
    hid = h_ref[...]
    window = jnp.exp(-dl_ref[:, 0:1] * pos_ref[0:1, :])
    if both:
        filt = _dot3(w4_ref[0], hid) * pos_ref[1:2, :] + _dot3(w4_ref[1], hid) * pos_ref[2:3, :]
    else:
        filt = _dot3(w4_ref[0], hid) * (pos_ref[1:2, :] + pos_ref[2:3, :])
    filt = filt * window + dl_ref[:, 1:2] * pos_ref[3:4, :]
    o_ref[...] = filt.reshape(o_ref.shape)


def _hyena_filter(feats, pos, length, w1t, w2t, w3t, bfr, w4t, decay_skip):
    n = feats.shape[1]
    nt = min(n, 2048)
    full = lambda shape: pl.BlockSpec(shape, lambda j: tuple(0 for _ in shape))
    hid = pl.pallas_call(
        _filt_hidden_kernel,
        grid=(n // nt,),
        in_specs=[pl.BlockSpec((HY_FILT_W, nt), lambda j: (0, j)),
                  full((HY_FILT_W, HY_FILT_W)), full((HY_FILT_W, HY_FILT_W)),
                  full((HY_FILT_W, HY_FILT_W)), full((HY_FILT_W, 4))],
        out_specs=pl.BlockSpec((HY_FILT_W, nt), lambda j: (0, j)),
        out_shape=jax.ShapeDtypeStruct((HY_FILT_W, n), F32),
        compiler_params=_cp(("arbitrary",)),
        name="hyena_filter_hidden",
    )(feats, w1t, w2t, w3t, bfr)
    ct = 256
    tiles = n // nt
    both = tiles == 1
    assert both or n == 2 * length
    w4_spec = (pl.BlockSpec((2, ct, HY_FILT_W), lambda c, j: (0, c, 0)) if both else
               pl.BlockSpec((1, ct, HY_FILT_W), lambda c, j: (j // (tiles // 2), c, 0)))
    return pl.pallas_call(
        functools.partial(_filt_kernel, both=both),
        grid=(HY_W // ct, tiles),
        in_specs=[pl.BlockSpec((HY_FILT_W, nt), lambda c, j: (0, j)),
                  w4_spec,
                  pl.BlockSpec((8, nt), lambda c, j: (0, j)),
                  pl.BlockSpec((ct, 2), lambda c, j: (c, 0))],
        out_specs=pl.BlockSpec((ct, nt // LANES, LANES), lambda c, j: (c, j, 0)),
        out_shape=jax.ShapeDtypeStruct((HY_W, n // LANES, LANES), F32),
        compiler_params=_cp(("parallel", "arbitrary")),
        name="hyena_filter",
    )(hid, w4t, pos, decay_skip)


def _hyconv_kernel(u_ref, f_ref, g1_ref, g1f_ref, g1i_ref, tw_ref, w3_ref, w3c_ref,
                   o_ref, ub_ref, l3_ref, l3f_ref, hs_ref, bp_ref, *, n1, r, ct):
    twr = tw_ref[0]
    twi = tw_ref[1]
    for bb in range(2):
        ub_ref[bb] = u_ref[bb].astype(F32).reshape(ct, r, LANES)

    def twiddled_rows(dst_ref, col, c):
        cr, ci = col[:n1], col[n1:]
        dst_ref[c * n1:(c + 1) * n1, 0:LANES] = (cr * twr - ci * twi).astype(BF)
        dst_ref[c * n1:(c + 1) * n1, LANES:2 * LANES] = (cr * twi + ci * twr).astype(BF)

    def column_dfts(c):
        rhs = jnp.concatenate(
            [jnp.concatenate([ub_ref[0, c + j], ub_ref[1, c + j]], axis=0) for j in range(2)],
            axis=1).astype(BF)
        bcol = jnp.dot(g1_ref[...], rhs, preferred_element_type=F32)
        for j in range(2):
            twiddled_rows(l3_ref, bcol[:, j * LANES:(j + 1) * LANES], c + j)

    def filter_column_dfts(c):
        frhs = jnp.concatenate([f_ref[c], f_ref[c + 1]], axis=1).astype(BF)
        fcol = jnp.dot(g1f_ref[...], frhs, preferred_element_type=F32)
        for j in range(2):
            twiddled_rows(l3f_ref, fcol[:, j * LANES:(j + 1) * LANES], c + j)

    rows_b = 256

    def filter_spectrum(k):
        rows = slice(k * rows_b, (k + 1) * rows_b)
        hs_ref[rows, :] = jnp.dot(l3f_ref[rows, :], w3_ref[...], preferred_element_type=F32)

    def spectrum_product(k):
        rows = slice(k * rows_b, (k + 1) * rows_b)
        xs = jnp.dot(l3_ref[rows, :], w3_ref[...], preferred_element_type=F32)
        xr, xi = xs[:, :LANES], xs[:, LANES:]
        hr, hi = hs_ref[rows, 0:LANES], hs_ref[rows, LANES:2 * LANES]
        ys = jnp.concatenate([xr * hr - xi * hi, xr * hi + xi * hr], axis=1).astype(BF)
        bp_ref[rows, :] = jnp.dot(ys, w3c_ref[...], preferred_element_type=F32)

    def untwiddled(c):
        bp = bp_ref[c * n1:(c + 1) * n1, :]
        br, bi = bp[:, :LANES], bp[:, LANES:]
        return jnp.concatenate([br * twr + bi * twi, bi * twr - br * twi], axis=0)

    def inverse_column_dfts(c):
        rhs = jnp.concatenate([untwiddled(c), untwiddled(c + 1)], axis=1).astype(BF)
        out = jnp.dot(g1i_ref[...], rhs, preferred_element_type=F32)
        for j in range(2):
            ub_ref[0, c + j] = out[:r, j * LANES:(j + 1) * LANES]
            ub_ref[1, c + j] = out[r:, j * LANES:(j + 1) * LANES]

    for c in range(0, ct, 2):
        filter_column_dfts(c)
    for k in range(ct * n1 // rows_b):
        filter_spectrum(k)
    for c in range(0, ct, 2):
        column_dfts(c)
    for k in range(ct * n1 // rows_b):
        spectrum_product(k)
    for c in range(0, ct, 2):
        inverse_column_dfts(c)

    for bb in range(2):
        o_ref[bb] = ub_ref[bb].reshape(ct, r * LANES).astype(o_ref.dtype)


@functools.lru_cache(maxsize=None)
def _dft_constants(n1, r):
    n = n1 * LANES
    k1 = np.arange(n1)
    f1 = np.exp(-2j * np.pi * np.outer(k1, k1) / n1)
    f2 = np.exp(-2j * np.pi * np.outer(np.arange(LANES), np.arange(LANES)) / LANES)
    tw = np.exp(-2j * np.pi * np.outer(k1, np.arange(LANES)) / n)
    g1 = np.block([[f1.real[:, :r], -f1.imag[:, :r]], [f1.imag[:, :r], f1.real[:, :r]]])
    g1f = np.concatenate([f1.real, f1.imag], axis=0)
    g1i = np.block([[f1.real[:r], f1.imag[:r]], [-f1.imag[:r], f1.real[:r]]]) / n
    w3 = np.block([[f2.real, f2.imag], [-f2.imag, f2.real]])
    w3c = np.block([[f2.real, -f2.imag], [f2.imag, f2.real]])
    tws = np.stack([tw.real, tw.imag])
    return tuple(np.asarray(a, np.float32) for a in (g1, g1f, g1i, tws, w3, w3c))


def _hyconv(u_t, filt, *, n1, r, tok_block):
    b, c_all = u_t.shape[0], u_t.shape[1]
    ct = 32 if n1 >= 128 else 64
    g1, g1f, g1i, tws, w3, w3c = (jnp.asarray(a) for a in _dft_constants(n1, r))
    g1, g1f, g1i, w3, w3c = (a.astype(BF) for a in (g1, g1f, g1i, w3, w3c))
    sig = pl.BlockSpec((b, ct, r * LANES), lambda i: (0, i, tok_block))
    full = lambda a: pl.BlockSpec(a.shape, lambda i: tuple(0 for _ in a.shape))
    return pl.pallas_call(
        functools.partial(_hyconv_kernel, n1=n1, r=r, ct=ct),
        grid=(c_all // ct,),
        in_specs=[sig, pl.BlockSpec((ct, n1, LANES), lambda i: (i, 0, 0)),
                  full(g1), full(g1f), full(g1i), full(tws), full(w3), full(w3c)],
        out_specs=pl.BlockSpec((b, ct, r * LANES), lambda i: (0, i, 0)),
        out_shape=jax.ShapeDtypeStruct((b, c_all, r * LANES), ACT),
        scratch_shapes=[pltpu.VMEM((b, ct, r, LANES), F32),
                        pltpu.VMEM((ct * n1, 2 * LANES), BF), pltpu.VMEM((ct * n1, 2 * LANES), BF),
                        pltpu.VMEM((ct * n1, 2 * LANES), F32), pltpu.VMEM((ct * n1, 2 * LANES), F32)],
        compiler_params=_cp(("arbitrary",)),
        name="hyena_conv_n%d" % n1,
    )(u_t, filt, g1, g1f, g1i, tws, w3, w3c)


def _merge_kernel(*refs, final, row_major):
    yf_ref, yb_ref, zs_ref, g0_ref, yh_ref, *refs = refs
    if final:
        y_h = yh_ref[0]
    else:
        yhc_ref, *refs = refs
        y_h = jnp.where(pl.program_id(1) < LAT_TILES, yh_ref[0], yhc_ref[0])
    if row_major:
        x_ref, ctx_ref, mod_ref, gw_ref, w_ref, fw_ref, o_ref = refs
        x = jnp.where(pl.program_id(1) < LAT_TILES, x_ref[0], ctx_ref[0]).T
    else:
        x_ref, mod_ref, gw_ref, w_ref, fw_ref, o_ref = refs
        x = x_ref[0]
    gate = mod_ref[0, 0, :, 2:3]

    def projected_group(g, cols):
        rs = lax.rsqrt(jnp.mean(g * g, axis=0, keepdims=True) + EPS)
        scaled = (g * gw_ref[cols, :]).astype(BF)
        return jnp.dot(w_ref[:, cols], scaled, preferred_element_type=F32) * rs

    g_s = (yf_ref[0].astype(F32) + yb_ref[0].astype(F32)) * zs_ref[0].astype(F32)
    g_h = y_h.astype(F32) * g0_ref[0].astype(F32)
    out = projected_group(g_s, slice(0, SSD_W)) + projected_group(g_h, slice(SSD_W, SSD_W + HY_W))
    xn = x + gate * out
    if final:
        ms = jnp.mean(xn * xn, axis=0, keepdims=True)
        o_ref[0] = ((xn * lax.rsqrt(ms + EPS)) * fw_ref[...]).T
    else:
        o_ref[0] = xn


def _merge(yf, yb, zs, g0, yh, stream, mod, gw, w_out_t, layer, fw, *, final, row_major):
    b = yf.shape[0]
    tiles = LAT_TILES if final else ALL_TILES
    tok = lambda i, t: (i, 0, t)
    const2 = lambda i, t: (0, 0)
    act = lambda rows: pl.BlockSpec((1, rows, TT), tok)
    if final:
        out_spec = pl.BlockSpec((1, TT, D_MODEL), lambda i, t: (i, t, 0))
        out_shape = jax.ShapeDtypeStruct((b, SEQ, D_MODEL), F32)
    else:
        out_spec = act(D_MODEL)
        out_shape = jax.ShapeDtypeStruct((b, D_MODEL, T_ALL), F32)
    yh_specs = [pl.BlockSpec((1, HY_W, TT), lambda i, t: (i, 0, jnp.minimum(t, LAT_TILES - 1)))]
    if not final:
        yh_specs.append(pl.BlockSpec((1, HY_W, TT), lambda i, t: (i, 0, 0)))
    return pl.pallas_call(
        functools.partial(_merge_kernel, final=final, row_major=row_major),
        grid=(b, tiles),
        in_specs=[act(SSD_W), act(SSD_W), act(SSD_W), act(HY_W)] + yh_specs + _stream_specs(row_major) + [
                  pl.BlockSpec((1, 1, D_MODEL, 4), _mod_kind),
                  pl.BlockSpec((SSD_W + HY_W, 1), const2),
                  pl.BlockSpec((None, D_MODEL, SSD_W + HY_W), lambda i, t: (layer, 0, 0)),
                  pl.BlockSpec((D_MODEL, 1), const2)],
        out_specs=out_spec,
        out_shape=out_shape,
        compiler_params=_cp(("parallel", "arbitrary")),
        name="merge_out_projection",
    )(yf, yb, zs, g0, *yh, *stream, mod, gw, w_out_t, fw)


@functools.lru_cache(maxsize=None)
def _conv_masks():
    t = np.arange(T_ALL)
    pos = np.where(t < SEQ, t % GRID_W, t - SEQ)
    period = np.where(t < SEQ, GRID_W, CTX_LEN)
    ml = (pos != 0).astype(np.float32)[None, :]
    mr = (pos != period - 1).astype(np.float32)[None, :]
    return ml, mr


@functools.lru_cache(maxsize=None)
def _filter_positions(n, length):
    idx = np.arange(n)
    fwd = idx < length
    bwd = idx > n - length
    d = np.where(fwd, idx, np.where(bwd, n - idx, 0)).astype(np.float64)
    t = (np.linspace(0.0, 1.0, length, dtype=np.float32).astype(np.float64))[d.astype(np.int64)]
    w = (2.0 * math.pi / length) * d
    f = np.linspace(1e-4, HY_BANDS - 1, HY_BANDS, dtype=np.float32).astype(np.float64)[:, None]
    feats = np.zeros((HY_FILT_W, n), np.float32)
    feats[0] = t
    feats[1:1 + HY_BANDS] = np.cos(f * w[None, :])
    feats[1 + HY_BANDS:HY_EMB] = -np.sin(f * w[None, :])
    pos = np.zeros((8, n), np.float32)
    pos[0] = t
    pos[1] = fwd
    pos[2] = bwd
    pos[3] = idx == 0
    return feats, pos


def kernel(x, c, ctx, c_ctx, norm_w, ada_w, ada_b, w_in, ssd_conv_w, ssd_conv_b, dt_bias, a_log, d_skip,
           hy_conv_w, hy_conv_b, filt_w1, filt_b1, filt_w2, filt_b2, filt_w3, filt_b3, filt_w4, filt_freq,
           hy_bias, gnorm_w, w_out, final_norm_w):
    bsz = x.shape[0]
    depth = norm_w.shape[0]
    assert x.shape == (bsz, SEQ, D_MODEL) and ctx.shape == (bsz, CTX_LEN, D_MODEL) and bsz == 2

    s_rows = jnp.zeros((8, D_MODEL), F32).at[:bsz].set(c).at[bsz].set(c_ctx)
    mods = _ada(s_rows, ada_w, ada_b)
    mods = mods[:, :bsz + 1].reshape(depth, bsz + 1, 3, D_MODEL)
    lat = mods[:, :bsz]
    cx = jnp.broadcast_to(mods[:, bsz:bsz + 1], lat.shape)
    mod = jnp.stack([lat, cx], axis=2)
    mod = jnp.stack([mod[:, :, :, 0], 1.0 + mod[:, :, :, 1], mod[:, :, :, 2], jnp.zeros_like(mod[:, :, :, 0])],
                    axis=-1)
    mod_rows = jnp.swapaxes(mod, -1, -2)

    ml, mr = (jnp.asarray(a) for a in _conv_masks())
    deltas = jnp.abs(jnp.linspace(HY_MIN_DECAY, HY_MAX_DECAY, HY_W, dtype=F32))

    w_in_t = jnp.swapaxes(w_in, 1, 2).astype(BF)
    w_out_t = jnp.swapaxes(w_out, 1, 2).astype(BF)

    stream = (x, ctx)
    out = None
    for l in range(depth):
        last = l == depth - 1
        row_major = l == 0
        cw = jnp.concatenate([
            jnp.concatenate([ssd_conv_w[l].T, ssd_conv_b[l][:, None]], axis=1),
            jnp.concatenate([hy_conv_w[l].T, hy_conv_b[l][:, None]], axis=1)], axis=0)
        par = jnp.stack([dt_bias[l].reshape(-1), a_log[l].reshape(-1)], axis=1)
        if row_major:
            vecs = (mod_rows[l], norm_w[l][None, :])
        else:
            vecs = (mod[l], norm_w[l][:, None])
        zs, xbc, da, cc, g0, u = _inproj(stream, *vecs, w_in_t, l, cw, par, ml, mr, row_major=row_major)

        dexp = jnp.broadcast_to(jnp.repeat(d_skip[l], SSD_HEADDIM, axis=1)[:, :, None],
                                (2, SSD_W, CHUNK))
        yf, yb = _ssd(xbc, da, cc, dexp)

        w4t = filt_w4[l].T.reshape(2, HY_W, HY_FILT_W)
        bfr = jnp.stack([filt_b1[l], filt_b2[l], filt_b3[l], filt_freq[l]], axis=1)
        w1t = jnp.pad(filt_w1[l].T, ((0, 0), (0, HY_FILT_W - HY_EMB)))
        decay_skip = jnp.stack([deltas, hy_bias[l]], axis=1)

        def long_conv(n1, r, tok_block, length):
            feats, pos = (jnp.asarray(a) for a in _filter_positions(n1 * LANES, length))
            filt = _hyena_filter(feats, pos, length, w1t, filt_w2[l].T, filt_w3[l].T, bfr, w4t, decay_skip)
            return _hyconv(u, filt, n1=n1, r=r, tok_block=tok_block)

        yh = (long_conv(2 * SEQ // LANES, SEQ // LANES, 0, SEQ),)
        if not last:
            yh += (long_conv(16, HY_CTX_ROWS, SEQ // (HY_CTX_ROWS * LANES), CTX_LEN),)

        res = _merge(yf, yb, zs, g0, yh, stream, mod[l], gnorm_w[l][:, None], w_out_t, l,
                     final_norm_w[:, None], final=last, row_major=row_major)
        if last:
            out = res
        else:
            stream = (res,)
    return out
```

```python
import functools
import math

import numpy as np
import jax
import jax.numpy as jnp
from jax import lax
from jax.experimental import pallas as pl
from jax.experimental.pallas import tpu as pltpu

D_MODEL = 1024
SEQ = 8192
CTX_LEN = 256
GRID_W = 64
SSD_W = 1024
HY_W = 1024
SSD_HEADDIM = 64
SSD_HEADS = SSD_W // SSD_HEADDIM
SSD_GROUPS = 2
SSD_STATE = 128
CHUNK = 128
HY_EMB = 33
HY_BANDS = (HY_EMB - 1) // 2
HY_FILT_W = 64
HY_TARGET = 1e-2
HY_MIN_DECAY = math.log(HY_TARGET) / 1.5
HY_MAX_DECAY = math.log(HY_TARGET) / 0.3
EPS = 1e-6
XBC_W = SSD_W + 2 * SSD_GROUPS * SSD_STATE
COL_ZS = 0
COL_XBC = COL_ZS + SSD_W
COL_DT = COL_XBC + XBC_W
COL_ZH = COL_DT + 2 * SSD_HEADS
COL_HY = COL_ZH + HY_W

LANES = 128
T_ALL = SEQ + CTX_LEN
TT = 256
LAT_TILES = SEQ // TT
ALL_TILES = T_ALL // TT
HY_CTX_ROWS = 8
T_HY = SEQ + HY_CTX_ROWS * LANES
HY_TILES = T_HY // TT
DT_ROWS = 128
DA_ROWS = 5 * 2 * SSD_HEADS
SSD_BLOCK = 2 * CHUNK
R_ZS = COL_ZS
R_XBC = COL_XBC
R_DT = COL_DT
R_ZH = COL_ZH
R_HY = COL_HY
R_END = COL_HY + 3 * HY_W
CONV_ROWS = XBC_W + 3 * HY_W

ACT = jnp.bfloat16
BF = jnp.bfloat16
F32 = jnp.float32
HI = lax.Precision.HIGHEST
VMEM_LIMIT = 56 * 1024 * 1024


def _cp(sem, flags=None):
    return pltpu.CompilerParams(dimension_semantics=sem, vmem_limit_bytes=VMEM_LIMIT, flags=flags)


def _silu(v):
    return v * jax.nn.sigmoid(v)


def _ada_kernel(s_ref, w_ref, b_ref, o_ref):
    s = _silu(s_ref[...])
    o_ref[0] = jnp.dot(s, w_ref[0], precision=HI, preferred_element_type=F32) + b_ref[0]


def _ada(s_rows, ada_w, ada_b):
    depth, d, n3 = ada_w.shape
    ct = 1024
    return pl.pallas_call(
        _ada_kernel,
        grid=(depth, n3 // ct),
        in_specs=[
            pl.BlockSpec((8, d), lambda l, j: (0, 0)),
            pl.BlockSpec((1, d, ct), lambda l, j: (l, 0, j)),
            pl.BlockSpec((1, 1, ct), lambda l, j: (l, 0, j)),
        ],
        out_specs=pl.BlockSpec((1, 8, ct), lambda l, j: (l, 0, j)),
        out_shape=jax.ShapeDtypeStruct((depth, 8, n3), F32),
        compiler_params=_cp(("arbitrary", "arbitrary")),
        name="ada_modulation",
    )(s_rows, ada_w, ada_b.reshape(depth, 1, n3))


def _inproj_kernel(*refs, row_major):
    if row_major:
        x_ref, ctx_ref, *refs = refs
    else:
        x_ref, *refs = refs
    (mod_ref, nw_ref, w_ref, cw_ref, par_ref, ml_ref, mr_ref,
     zs_ref, xbc_ref, da_ref, cc_ref, g0_ref, u_ref, cwb_ref) = refs
    t = pl.program_id(1)
    rc = 256

    @pl.when(t == 0)
    def _():
        for r in range(0, CONV_ROWS, 512):
            for k in range(4):
                cwb_ref[k, r:r + 512, :] = jnp.broadcast_to(cw_ref[r:r + 512, k:k + 1], (512, LANES))

    @pl.when(t < ALL_TILES)
    def _():
        if row_major:
            x = jnp.where(t < LAT_TILES, x_ref[0], ctx_ref[0])
            ms = jnp.mean(x * x, axis=1, keepdims=True)
            xn = x * lax.rsqrt(ms + EPS)
            xb = ((xn * nw_ref[...]) * mod_ref[0, 0, 1:2, :] + mod_ref[0, 0, 0:1, :]).astype(BF)
            dims = (((1,), (1,)), ((), ()))
        else:
            x = x_ref[0]
            ms = jnp.mean(x * x, axis=0, keepdims=True)
            xn = x * lax.rsqrt(ms + EPS)
            xb = ((xn * nw_ref[...]) * mod_ref[0, 0, :, 1:2] + mod_ref[0, 0, :, 0:1]).astype(BF)
            dims = (((1,), (0,)), ((), ()))
        ml = ml_ref[...]
        mr = mr_ref[...]

        def mm(r0, rows):
            return lax.dot_general(w_ref[r0:r0 + rows, :], xb, dims, preferred_element_type=F32)

        def conv(p, r0, rows):
            left = pltpu.roll(p, 1, axis=1) * ml
            right = pltpu.roll(p, TT - 1, axis=1) * mr
            rs = slice(r0, r0 + rows)
            halves = []
            for hl in range(0, TT, LANES):
                ls = slice(hl, hl + LANES)
                halves.append(cwb_ref[0, rs, :] * left[:, ls] + cwb_ref[1, rs, :] * p[:, ls]
                              + cwb_ref[2, rs, :] * right[:, ls] + cwb_ref[3, rs, :])
            return jnp.concatenate(halves, axis=1)

        for r in range(0, SSD_W, rc):
            zs_ref[0, r:r + rc, :] = _silu(mm(R_ZS + r, rc)).astype(zs_ref.dtype)
        for r in range(0, XBC_W, rc):
            xbc_ref[0, r:r + rc, :] = _silu(conv(mm(R_XBC + r, rc), r, rc)).astype(xbc_ref.dtype)
        nh = SSD_HEADS
        dt = jax.nn.softplus(mm(R_DT, DT_ROWS)[0:2 * nh, :] + par_ref[:, 0:1])
        a = dt * -jnp.exp(par_ref[:, 1:2])
        pos = lax.broadcasted_iota(jnp.int32, a.shape, 1) & (CHUNK - 1)
        pre = a
        suf = a
        k = 1
        while k < CHUNK:
            pre = pre + jnp.where(pos >= k, pltpu.roll(pre, k, axis=1), 0.0)
            suf = suf + jnp.where(pos < CHUNK - k, pltpu.roll(suf, TT - k, axis=1), 0.0)
            k *= 2
        tot = pre + suf - a
        cum = jnp.concatenate([pre[0:nh], suf[nh:2 * nh]], axis=0)
        da_ref[0] = jnp.concatenate([dt, cum, jnp.exp(tot - cum), jnp.exp(cum), jnp.exp(tot)], axis=0)
        for hc in range(TT // CHUNK):
            sq = jnp.concatenate([cum[:, hc * CHUNK:(hc + 1) * CHUNK],
                                  jnp.zeros((LANES - 2 * nh, CHUNK), F32)], axis=0)
            cc_ref[0, hc * CHUNK:(hc + 1) * CHUNK, :] = sq.T
        for r in range(0, HY_W, rc):
            gate = _silu(mm(R_ZH + r, rc)) * conv(mm(R_HY + r, rc), XBC_W + r, rc)
            g0_ref[0, r:r + rc, :] = gate.astype(g0_ref.dtype)
            p1 = conv(mm(R_HY + HY_W + r, rc), XBC_W + HY_W + r, rc)
            pv = conv(mm(R_HY + 2 * HY_W + r, rc), XBC_W + 2 * HY_W + r, rc)
            u_ref[0, r:r + rc, :] = (p1 * pv).astype(u_ref.dtype)

    @pl.when(t >= ALL_TILES)
    def _():
        u_ref[...] = jnp.zeros_like(u_ref)


def _mod_kind(i, t):
    return (i, jnp.where(t >= LAT_TILES, 1, 0), 0, 0)


def _stream_specs(row_major):
    if row_major:
        return [pl.BlockSpec((1, TT, D_MODEL), lambda i, t: (i, jnp.minimum(t, LAT_TILES - 1), 0)),
                pl.BlockSpec((1, CTX_LEN, D_MODEL), lambda i, t: (i, 0, 0))]
    return [pl.BlockSpec((1, D_MODEL, TT), lambda i, t: (i, 0, jnp.minimum(t, ALL_TILES - 1)))]


def _inproj(stream, mod, nw, w_t, layer, cw, par, ml, mr, *, row_major):
    b = stream[0].shape[0]
    vec_specs = [pl.BlockSpec((1, 1) + mod.shape[2:], _mod_kind), pl.BlockSpec(nw.shape, lambda i, t: (0, 0))]
    last = ALL_TILES - 1
    tok = lambda i, t: (i, 0, jnp.minimum(t, last))
    const2 = lambda i, t: (0, 0)
    out_tok = lambda rows: pl.BlockSpec((1, rows, TT), tok)
    return pl.pallas_call(
        functools.partial(_inproj_kernel, row_major=row_major),
        grid=(b, HY_TILES),
        in_specs=_stream_specs(row_major) + vec_specs + [
            pl.BlockSpec((None, R_END, D_MODEL), lambda i, t: (layer, 0, 0), pipeline_mode=pl.Buffered(1)),
            pl.BlockSpec((CONV_ROWS, 4), const2),
            pl.BlockSpec((2 * SSD_HEADS, 2), const2),
            pl.BlockSpec((1, TT), lambda i, t: (0, jnp.minimum(t, last))),
            pl.BlockSpec((1, TT), lambda i, t: (0, jnp.minimum(t, last))),
        ],
        out_specs=[
            out_tok(SSD_W), out_tok(XBC_W), out_tok(DA_ROWS),
            pl.BlockSpec((1, TT, LANES), lambda i, t: (i, jnp.minimum(t, last), 0)),
            out_tok(HY_W),
            pl.BlockSpec((1, HY_W, TT), lambda i, t: (i, 0, t)),
        ],
        out_shape=[
            jax.ShapeDtypeStruct((b, SSD_W, T_ALL), ACT),
            jax.ShapeDtypeStruct((b, XBC_W, T_ALL), ACT),
            jax.ShapeDtypeStruct((b, DA_ROWS, T_ALL), F32),
            jax.ShapeDtypeStruct((b, T_ALL, LANES), F32),
            jax.ShapeDtypeStruct((b, HY_W, T_ALL), ACT),
            jax.ShapeDtypeStruct((b, HY_W, T_HY), ACT),
        ],
        scratch_shapes=[pltpu.VMEM((4, CONV_ROWS, LANES), F32)],
        compiler_params=_cp(("parallel", "arbitrary")),
        name="in_projection",
    )(*stream, mod, nw, w_t, cw, par, ml, mr)


def _ssd_direction(d, ck, x_ref, da_ref, cc_ref, dexp_ref, y_ref, st_ref, xdt_ref, w_ref, valid):
    nh = SSD_HEADS
    hd = SSD_HEADDIM
    hpg = nh // SSD_GROUPS
    o = d * nh
    tk = slice(ck * CHUNK, (ck + 1) * CHUNK)
    field = lambda k: da_ref[0, 2 * k * nh + o:2 * k * nh + o + nh, tk]
    dt, cum_t, e_end, e_in, e_tot = (field(k) for k in range(5))
    cum_c = cc_ref[0, tk, o:o + nh]
    per_head = lambda v, h: jnp.broadcast_to(v[h:h + 1, :], (hd, CHUNK)).astype(BF)
    for h in range(nh):
        rows = slice(h * hd, (h + 1) * hd)
        xdt = x_ref[0, rows, tk].astype(BF) * per_head(dt, h)
        xdt_ref[rows, :] = xdt
        w_ref[rows, :] = xdt * per_head(e_end, h)
    for g in range(SSD_GROUPS):
        grows = slice(g * hpg * hd, (g + 1) * hpg * hd)
        b_t = x_ref[0, SSD_W + g * SSD_STATE:SSD_W + (g + 1) * SSD_STATE, tk].astype(BF)
        c_t = x_ref[0, SSD_W + (SSD_GROUPS + g) * SSD_STATE:
                    SSD_W + (SSD_GROUPS + g + 1) * SSD_STATE, tk].astype(BF)
        cb_t = lax.dot_general(b_t, c_t, (((0,), (0,)), ((), ())), preferred_element_type=F32).astype(BF)
        inter = jnp.dot(st_ref[grows, :].astype(BF), c_t, preferred_element_type=F32)
        s_new = lax.dot_general(w_ref[grows, :], b_t, (((1,), (1,)), ((), ())),
                                preferred_element_type=F32)
        for h in range(g * hpg, (g + 1) * hpg):
            rows = slice(h * hd, (h + 1) * hd)
            loc = slice((h - g * hpg) * hd, (h - g * hpg + 1) * hd)
            seg = cum_t[h:h + 1, :] - cum_c[:, h:h + 1]
            m_t = cb_t * jnp.exp(jnp.where(valid, seg, -jnp.inf)).astype(BF)
            intra = jnp.dot(xdt_ref[rows, :], m_t, preferred_element_type=F32)
            y_ref[0, rows, tk] = (intra + inter[loc] * e_in[h:h + 1, :]
                                  + dexp_ref[d, rows, :] * x_ref[0, rows, tk].astype(F32)).astype(y_ref.dtype)
            st_ref[rows, :] = e_tot[h:h + 1, :] * st_ref[rows, :] + s_new[loc]


def _ssd_kernel(xf_ref, xb_ref, daf_ref, dab_ref, ccf_ref, ccb_ref, dexp_ref, yf_ref, yb_ref,
                sf_ref, sb_ref, xdtf_ref, wf_ref, xdtb_ref, wb_ref):
    @pl.when(pl.program_id(1) == 0)
    def _():
        sf_ref[...] = jnp.zeros_like(sf_ref)
        sb_ref[...] = jnp.zeros_like(sb_ref)

    row = lax.broadcasted_iota(jnp.int32, (CHUNK, CHUNK), 0)
    col = lax.broadcasted_iota(jnp.int32, (CHUNK, CHUNK), 1)
    per = SSD_BLOCK // CHUNK
    for k in range(per):
        _ssd_direction(0, k, xf_ref, daf_ref, ccf_ref, dexp_ref, yf_ref, sf_ref, xdtf_ref, wf_ref,
                       row <= col)
        _ssd_direction(1, per - 1 - k, xb_ref, dab_ref, ccb_ref, dexp_ref, yb_ref, sb_ref, xdtb_ref, wb_ref,
                       row >= col)


def _ssd(xbc_t, da_t, cc, dexp):
    b = xbc_t.shape[0]
    nb = T_ALL // SSD_BLOCK
    lat = SEQ // SSD_BLOCK
    fwd = lambda i, s: (i, 0, (s + lat) % nb)
    bwd = lambda i, s: (i, 0, nb - 1 - s)
    fwd_r = lambda i, s: (i, (s + lat) % nb, 0)
    bwd_r = lambda i, s: (i, nb - 1 - s, 0)
    return pl.pallas_call(
        _ssd_kernel,
        grid=(b, nb),
        in_specs=[
            pl.BlockSpec((1, XBC_W, SSD_BLOCK), fwd),
            pl.BlockSpec((1, XBC_W, SSD_BLOCK), bwd),
            pl.BlockSpec((1, DA_ROWS, SSD_BLOCK), fwd),
            pl.BlockSpec((1, DA_ROWS, SSD_BLOCK), bwd),
            pl.BlockSpec((1, SSD_BLOCK, LANES), fwd_r),
            pl.BlockSpec((1, SSD_BLOCK, LANES), bwd_r),
            pl.BlockSpec((2, SSD_W, CHUNK), lambda i, s: (0, 0, 0)),
        ],
        out_specs=[pl.BlockSpec((1, SSD_W, SSD_BLOCK), fwd), pl.BlockSpec((1, SSD_W, SSD_BLOCK), bwd)],
        out_shape=[jax.ShapeDtypeStruct((b, SSD_W, T_ALL), ACT)] * 2,
        scratch_shapes=[pltpu.VMEM((SSD_W, SSD_STATE), F32)] * 2 + [pltpu.VMEM((SSD_W, CHUNK), BF)] * 4,
        compiler_params=_cp(("parallel", "arbitrary")),
        name="ssd_scan",
    )(xbc_t, xbc_t, da_t, da_t, cc, cc, dexp)


def _split_bf16(v):
    hi = v.astype(BF)
    return hi, (v - hi.astype(F32)).astype(BF)


def _dot3(a, b):
    a_hi, a_lo = _split_bf16(a)
    b_hi, b_lo = _split_bf16(b)
    return jnp.dot(jnp.concatenate([a_hi, a_hi, a_lo], axis=1),
                   jnp.concatenate([b_hi, b_lo, b_hi], axis=0), preferred_element_type=F32)


def _filt_hidden_kernel(f_ref, w1_ref, w2_ref, w3_ref, bf_ref, o_ref):
    fr = bf_ref[:, 3:4]
    hcur = jnp.sin(fr * (_dot3(w1_ref[...], f_ref[...]) + bf_ref[:, 0:1]))
    hcur = jnp.sin(fr * (_dot3(w2_ref[...], hcur) + bf_ref[:, 1:2]))
    o_ref[...] = jnp.sin(fr * (_dot3(w3_ref[...], hcur) + bf_ref[:, 2:3]))


def _filt_kernel(h_ref, w4_ref, pos_ref, dl_ref, o_ref, *, both):
    hid = h_ref[...]
    window = jnp.exp(-dl_ref[:, 0:1] * pos_ref[0:1, :])
    if both:
        filt = _dot3(w4_ref[0], hid) * pos_ref[1:2, :] + _dot3(w4_ref[1], hid) * pos_ref[2:3, :]
    else:
        filt = _dot3(w4_ref[0], hid) * (pos_ref[1:2, :] + pos_ref[2:3, :])
    filt = filt * window + dl_ref[:, 1:2] * pos_ref[3:4, :]
    o_ref[...] = filt.reshape(o_ref.shape)


def _hyena_filter(feats, pos, length, w1t, w2t, w3t, bfr, w4t, decay_skip):
    n = feats.shape[1]
    nt = min(n, 2048)
    full = lambda shape: pl.BlockSpec(shape, lambda j: tuple(0 for _ in shape))
    hid = pl.pallas_call(
        _filt_hidden_kernel,
        grid=(n // nt,),
        in_specs=[pl.BlockSpec((HY_FILT_W, nt), lambda j: (0, j)),
                  full((HY_FILT_W, HY_FILT_W)), full((HY_FILT_W, HY_FILT_W)),
                  full((HY_FILT_W, HY_FILT_W)), full((HY_FILT_W, 4))],
        out_specs=pl.BlockSpec((HY_FILT_W, nt), lambda j: (0, j)),
        out_shape=jax.ShapeDtypeStruct((HY_FILT_W, n), F32),
        compiler_params=_cp(("arbitrary",)),
        name="hyena_filter_hidden",
    )(feats, w1t, w2t, w3t, bfr)
    ct = 256
    tiles = n // nt
    both = tiles == 1
    assert both or n == 2 * length
    w4_spec = (pl.BlockSpec((2, ct, HY_FILT_W), lambda c, j: (0, c, 0)) if both else
               pl.BlockSpec((1, ct, HY_FILT_W), lambda c, j: (j // (tiles // 2), c, 0)))
    return pl.pallas_call(
        functools.partial(_filt_kernel, both=both),
        grid=(HY_W // ct, tiles),
        in_specs=[pl.BlockSpec((HY_FILT_W, nt), lambda c, j: (0, j)),
                  w4_spec,
                  pl.BlockSpec((8, nt), lambda c, j: (0, j)),
                  pl.BlockSpec((ct, 2), lambda c, j: (c, 0))],
        out_specs=pl.BlockSpec((ct, nt // LANES, LANES), lambda c, j: (c, j, 0)),
        out_shape=jax.ShapeDtypeStruct((HY_W, n // LANES, LANES), F32),
        compiler_params=_cp(("parallel", "arbitrary")),
        name="hyena_filter",
    )(hid, w4t, pos, decay_skip)


def _hyconv_kernel(u_ref, f_ref, g1_ref, g1f_ref, g1i_ref, tw_ref, w3_ref, w3c_ref,
                   o_ref, ub_ref, l3_ref, l3f_ref, hs_ref, bp_ref, *, n1, r, ct):
    twr = tw_ref[0].astype(BF)
    twi = tw_ref[1].astype(BF)
    for bb in range(2):
        ub_ref[bb] = u_ref[bb].astype(F32).reshape(ct, r, LANES)

    def twiddled_rows(dst_ref, col, c):
        cr, ci = col[:n1], col[n1:]
        dst_ref[c * n1:(c + 1) * n1, 0:LANES] = cr * twr - ci * twi
        dst_ref[c * n1:(c + 1) * n1, LANES:2 * LANES] = cr * twi + ci * twr

    def column_dfts(c):
        rhs = jnp.concatenate(
            [jnp.concatenate([ub_ref[0, c + j], ub_ref[1, c + j]], axis=0) for j in range(2)],
            axis=1).astype(BF)
        bcol = jnp.dot(g1_ref[...], rhs, preferred_element_type=F32).astype(BF)
        for j in range(2):
            twiddled_rows(l3_ref, bcol[:, j * LANES:(j + 1) * LANES], c + j)

    def filter_column_dfts(c):
        frhs = jnp.concatenate([f_ref[c], f_ref[c + 1]], axis=1).astype(BF)
        fcol = jnp.dot(g1f_ref[...], frhs, preferred_element_type=F32).astype(BF)
        for j in range(2):
            twiddled_rows(l3f_ref, fcol[:, j * LANES:(j + 1) * LANES], c + j)

    rows_b = 256

    def filter_spectrum(k):
        rows = slice(k * rows_b, (k + 1) * rows_b)
        hs_ref[rows, :] = jnp.dot(l3f_ref[rows, :], w3_ref[...],
                                  preferred_element_type=F32).astype(BF)

    def spectrum_product(k):
        rows = slice(k * rows_b, (k + 1) * rows_b)
        xs = jnp.dot(l3_ref[rows, :], w3_ref[...], preferred_element_type=F32).astype(BF)
        xr, xi = xs[:, :LANES], xs[:, LANES:]
        hr, hi = hs_ref[rows, 0:LANES], hs_ref[rows, LANES:2 * LANES]
        ys = jnp.concatenate([xr * hr - xi * hi, xr * hi + xi * hr], axis=1)
        bp_ref[rows, :] = jnp.dot(ys, w3c_ref[...], preferred_element_type=F32).astype(BF)

    def untwiddled(c):
        bp = bp_ref[c * n1:(c + 1) * n1, :]
        br, bi = bp[:, :LANES], bp[:, LANES:]
        return jnp.concatenate([br * twr + bi * twi, bi * twr - br * twi], axis=0)

    def inverse_column_dfts(c):
        rhs = jnp.concatenate([untwiddled(c), untwiddled(c + 1)], axis=1)
        out = jnp.dot(g1i_ref[...], rhs, preferred_element_type=F32)
        for j in range(2):
            ub_ref[0, c + j] = out[:r, j * LANES:(j + 1) * LANES]
            ub_ref[1, c + j] = out[r:, j * LANES:(j + 1) * LANES]

    for c in range(0, ct, 2):
        filter_column_dfts(c)
    for k in range(ct * n1 // rows_b):
        filter_spectrum(k)
    for c in range(0, ct, 2):
        column_dfts(c)
    for k in range(ct * n1 // rows_b):
        spectrum_product(k)
    for c in range(0, ct, 2):
        inverse_column_dfts(c)

    for bb in range(2):
        o_ref[bb] = ub_ref[bb].reshape(ct, r * LANES).astype(o_ref.dtype)


@functools.lru_cache(maxsize=None)
def _dft_constants(n1, r):
    n = n1 * LANES
    k1 = np.arange(n1)
    f1 = np.exp(-2j * np.pi * np.outer(k1, k1) / n1)
    f2 = np.exp(-2j * np.pi * np.outer(np.arange(LANES), np.arange(LANES)) / LANES)
    tw = np.exp(-2j * np.pi * np.outer(k1, np.arange(LANES)) / n)
    g1 = np.block([[f1.real[:, :r], -f1.imag[:, :r]], [f1.imag[:, :r], f1.real[:, :r]]])
    g1f = np.concatenate([f1.real, f1.imag], axis=0)
    g1i = np.block([[f1.real[:r], f1.imag[:r]], [-f1.imag[:r], f1.real[:r]]]) / n
    w3 = np.block([[f2.real, f2.imag], [-f2.imag, f2.real]])
    w3c = np.block([[f2.real, -f2.imag], [f2.imag, f2.real]])
    tws = np.stack([tw.real, tw.imag])
    return tuple(np.asarray(a, np.float32) for a in (g1, g1f, g1i, tws, w3, w3c))


def _hyconv(u_t, filt, *, n1, r, tok_block):
    b, c_all = u_t.shape[0], u_t.shape[1]
    ct = 32 if n1 >= 128 else 64
    g1, g1f, g1i, tws, w3, w3c = (jnp.asarray(a) for a in _dft_constants(n1, r))
    g1, g1f, g1i, w3, w3c = (a.astype(BF) for a in (g1, g1f, g1i, w3, w3c))
    sig = pl.BlockSpec((b, ct, r * LANES), lambda i: (0, i, tok_block))
    full = lambda a: pl.BlockSpec(a.shape, lambda i: tuple(0 for _ in a.shape))
    return pl.pallas_call(
        functools.partial(_hyconv_kernel, n1=n1, r=r, ct=ct),
        grid=(c_all // ct,),
        in_specs=[sig, pl.BlockSpec((ct, n1, LANES), lambda i: (i, 0, 0)),
                  full(g1), full(g1f), full(g1i), full(tws), full(w3), full(w3c)],
        out_specs=pl.BlockSpec((b, ct, r * LANES), lambda i: (0, i, 0)),
        out_shape=jax.ShapeDtypeStruct((b, c_all, r * LANES), ACT),
        scratch_shapes=[pltpu.VMEM((b, ct, r, LANES), F32),
                        pltpu.VMEM((ct * n1, 2 * LANES), BF), pltpu.VMEM((ct * n1, 2 * LANES), BF),
                        pltpu.VMEM((ct * n1, 2 * LANES), BF), pltpu.VMEM((ct * n1, 2 * LANES), BF)],
        compiler_params=_cp(("arbitrary",)),
        name="hyena_conv_n%d" % n1,
    )(u_t, filt, g1, g1f, g1i, tws, w3, w3c)


def _merge_kernel(*refs, final, row_major):
    yf_ref, yb_ref, zs_ref, g0_ref, yh_ref, *refs = refs
    if final:
        y_h = yh_ref[0]
    else:
        yhc_ref, *refs = refs
        y_h = jnp.where(pl.program_id(1) < LAT_TILES, yh_ref[0], yhc_ref[0])
    if row_major:
        x_ref, ctx_ref, mod_ref, gw_ref, w_ref, fw_ref, o_ref = refs
        x = jnp.where(pl.program_id(1) < LAT_TILES, x_ref[0], ctx_ref[0]).T
    else:
        x_ref, mod_ref, gw_ref, w_ref, fw_ref, o_ref = refs
        x = x_ref[0]
    gate = mod_ref[0, 0, :, 2:3]

    def projected_group(g, cols):
        rs = lax.rsqrt(jnp.mean(g * g, axis=0, keepdims=True) + EPS)
        scaled = (g * gw_ref[cols, :]).astype(BF)
        return jnp.dot(w_ref[:, cols], scaled, preferred_element_type=F32) * rs

    g_s = (yf_ref[0].astype(F32) + yb_ref[0].astype(F32)) * zs_ref[0].astype(F32)
    g_h = y_h.astype(F32) * g0_ref[0].astype(F32)
    out = projected_group(g_s, slice(0, SSD_W)) + projected_group(g_h, slice(SSD_W, SSD_W + HY_W))
    xn = x + gate * out
    if final:
        ms = jnp.mean(xn * xn, axis=0, keepdims=True)
        o_ref[0] = ((xn * lax.rsqrt(ms + EPS)) * fw_ref[...]).T
    else:
        o_ref[0] = xn


def _merge(yf, yb, zs, g0, yh, stream, mod, gw, w_out_t, layer, fw, *, final, row_major):
    b = yf.shape[0]
    tiles = LAT_TILES if final else ALL_TILES
    tok = lambda i, t: (i, 0, t)
    const2 = lambda i, t: (0, 0)
    act = lambda rows: pl.BlockSpec((1, rows, TT), tok)
    if final:
        out_spec = pl.BlockSpec((1, TT, D_MODEL), lambda i, t: (i, t, 0))
        out_shape = jax.ShapeDtypeStruct((b, SEQ, D_MODEL), F32)
    else:
        out_spec = act(D_MODEL)
        out_shape = jax.ShapeDtypeStruct((b, D_MODEL, T_ALL), F32)
    yh_specs = [pl.BlockSpec((1, HY_W, TT), lambda i, t: (i, 0, jnp.minimum(t, LAT_TILES - 1)))]
    if not final:
        yh_specs.append(pl.BlockSpec((1, HY_W, TT), lambda i, t: (i, 0, 0)))
    return pl.pallas_call(
        functools.partial(_merge_kernel, final=final, row_major=row_major),
        grid=(b, tiles),
        in_specs=[act(SSD_W), act(SSD_W), act(SSD_W), act(HY_W)] + yh_specs + _stream_specs(row_major) + [
                  pl.BlockSpec((1, 1, D_MODEL, 4), _mod_kind),
                  pl.BlockSpec((SSD_W + HY_W, 1), const2),
                  pl.BlockSpec((None, D_MODEL, SSD_W + HY_W), lambda i, t: (layer, 0, 0)),
                  pl.BlockSpec((D_MODEL, 1), const2)],
        out_specs=out_spec,
        out_shape=out_shape,
        compiler_params=_cp(("parallel", "arbitrary")),
        name="merge_out_projection",
    )(yf, yb, zs, g0, *yh, *stream, mod, gw, w_out_t, fw)


@functools.lru_cache(maxsize=None)
def _conv_masks():
    t = np.arange(T_ALL)
    pos = np.where(t < SEQ, t % GRID_W, t - SEQ)
    period = np.where(t < SEQ, GRID_W, CTX_LEN)
    ml = (pos != 0).astype(np.float32)[None, :]
    mr = (pos != period - 1).astype(np.float32)[None, :]
    return ml, mr


@functools.lru_cache(maxsize=None)
def _filter_positions(n, length):
    idx = np.arange(n)
    fwd = idx < length
    bwd = idx > n - length
    d = np.where(fwd, idx, np.where(bwd, n - idx, 0)).astype(np.float64)
    t = (np.linspace(0.0, 1.0, length, dtype=np.float32).astype(np.float64))[d.astype(np.int64)]
    w = (2.0 * math.pi / length) * d
    f = np.linspace(1e-4, HY_BANDS - 1, HY_BANDS, dtype=np.float32).astype(np.float64)[:, None]
    feats = np.zeros((HY_FILT_W, n), np.float32)
    feats[0] = t
    feats[1:1 + HY_BANDS] = np.cos(f * w[None, :])
    feats[1 + HY_BANDS:HY_EMB] = -np.sin(f * w[None, :])
    pos = np.zeros((8, n), np.float32)
    pos[0] = t
    pos[1] = fwd
    pos[2] = bwd
    pos[3] = idx == 0
    return feats, pos


def kernel(x, c, ctx, c_ctx, norm_w, ada_w, ada_b, w_in, ssd_conv_w, ssd_conv_b, dt_bias, a_log, d_skip,
           hy_conv_w, hy_conv_b, filt_w1, filt_b1, filt_w2, filt_b2, filt_w3, filt_b3, filt_w4, filt_freq,
           hy_bias, gnorm_w, w_out, final_norm_w):
    bsz = x.shape[0]
    depth = norm_w.shape[0]
    assert x.shape == (bsz, SEQ, D_MODEL) and ctx.shape == (bsz, CTX_LEN, D_MODEL) and bsz == 2

    s_rows = jnp.zeros((8, D_MODEL), F32).at[:bsz].set(c).at[bsz].set(c_ctx)
    mods = _ada(s_rows, ada_w, ada_b)
    mods = mods[:, :bsz + 1].reshape(depth, bsz + 1, 3, D_MODEL)
    lat = mods[:, :bsz]
    cx = jnp.broadcast_to(mods[:, bsz:bsz + 1], lat.shape)
    mod = jnp.stack([lat, cx], axis=2)
    mod = jnp.stack([mod[:, :, :, 0], 1.0 + mod[:, :, :, 1], mod[:, :, :, 2], jnp.zeros_like(mod[:, :, :, 0])],
                    axis=-1)
    mod_rows = jnp.swapaxes(mod, -1, -2)

    ml, mr = (jnp.asarray(a) for a in _conv_masks())
    deltas = jnp.abs(jnp.linspace(HY_MIN_DECAY, HY_MAX_DECAY, HY_W, dtype=F32))

    w_in_t = jnp.swapaxes(w_in, 1, 2).astype(BF)
    w_out_t = jnp.swapaxes(w_out, 1, 2).astype(BF)

    stream = (x, ctx)
    out = None
    for l in range(depth):
        last = l == depth - 1
        row_major = l == 0
        cw = jnp.concatenate([
            jnp.concatenate([ssd_conv_w[l].T, ssd_conv_b[l][:, None]], axis=1),
            jnp.concatenate([hy_conv_w[l].T, hy_conv_b[l][:, None]], axis=1)], axis=0)
        par = jnp.stack([dt_bias[l].reshape(-1), a_log[l].reshape(-1)], axis=1)
        if row_major:
            vecs = (mod_rows[l], norm_w[l][None, :])
        else:
            vecs = (mod[l], norm_w[l][:, None])
        zs, xbc, da, cc, g0, u = _inproj(stream, *vecs, w_in_t, l, cw, par, ml, mr, row_major=row_major)

        dexp = jnp.broadcast_to(jnp.repeat(d_skip[l], SSD_HEADDIM, axis=1)[:, :, None],
                                (2, SSD_W, CHUNK))
        yf, yb = _ssd(xbc, da, cc, dexp)

        w4t = filt_w4[l].T.reshape(2, HY_W, HY_FILT_W)
        bfr = jnp.stack([filt_b1[l], filt_b2[l], filt_b3[l], filt_freq[l]], axis=1)
        w1t = jnp.pad(filt_w1[l].T, ((0, 0), (0, HY_FILT_W - HY_EMB)))
        decay_skip = jnp.stack([deltas, hy_bias[l]], axis=1)

        def long_conv(n1, r, tok_block, length):
            feats, pos = (jnp.asarray(a) for a in _filter_positions(n1 * LANES, length))
            filt = _hyena_filter(feats, pos, length, w1t, filt_w2[l].T, filt_w3[l].T, bfr, w4t, decay_skip)
            return _hyconv(u, filt, n1=n1, r=r, tok_block=tok_block)

        yh = (long_conv(2 * SEQ // LANES, SEQ // LANES, 0, SEQ),)
        if not last:
            yh += (long_conv(16, HY_CTX_ROWS, SEQ // (HY_CTX_ROWS * LANES), CTX_LEN),)

        res = _merge(yf, yb, zs, g0, yh, stream, mod[l], gnorm_w[l][:, None], w_out_t, l,
                     final_norm_w[:, None], final=last, row_major=row_major)
        if last:
            out = res
        else:
            stream = (res,)
    return out
```

```python
import functools
import math

import numpy as np
import jax
import jax.numpy as jnp
from jax import lax
from jax.experimental import pallas as pl
from jax.experimental.pallas import tpu as pltpu

D_MODEL = 1024
SEQ = 8192
CTX_LEN = 256
GRID_W = 64
SSD_W = 1024
HY_W = 1024
SSD_HEADDIM = 64
SSD_HEADS = SSD_W // SSD_HEADDIM
SSD_GROUPS = 2
SSD_STATE = 128
CHUNK = 128
HY_EMB = 33
HY_BANDS = (HY_EMB - 1) // 2
HY_FILT_W = 64
HY_TARGET = 1e-2
HY_MIN_DECAY = math.log(HY_TARGET) / 1.5
HY_MAX_DECAY = math.log(HY_TARGET) / 0.3
EPS = 1e-6
XBC_W = SSD_W + 2 * SSD_GROUPS * SSD_STATE
COL_ZS = 0
COL_XBC = COL_ZS + SSD_W
COL_DT = COL_XBC + XBC_W
COL_ZH = COL_DT + 2 * SSD_HEADS
COL_HY = COL_ZH + HY_W

LANES = 128
T_ALL = SEQ + CTX_LEN
TT = 256
LAT_TILES = SEQ // TT
ALL_TILES = T_ALL // TT
HY_CTX_ROWS = 8
T_HY = SEQ + HY_CTX_ROWS * LANES
HY_TILES = T_HY // TT
DT_ROWS = 128
DA_ROWS = 5 * 2 * SSD_HEADS
SSD_BLOCK = 2 * CHUNK
R_ZS = COL_ZS
R_XBC = COL_XBC
R_DT = COL_DT
R_ZH = COL_ZH
R_HY = COL_HY
R_END = COL_HY + 3 * HY_W
CONV_ROWS = XBC_W + 3 * HY_W

ACT = jnp.bfloat16
BF = jnp.bfloat16
F32 = jnp.float32
VMEM_LIMIT = 56 * 1024 * 1024


def _cp(sem, flags=None):
    return pltpu.CompilerParams(dimension_semantics=sem, vmem_limit_bytes=VMEM_LIMIT, flags=flags)


def _silu(v):
    return v * jax.nn.sigmoid(v)


def _ada_kernel(s_ref, w_ref, b_ref, o_ref):
    s = _silu(s_ref[...])
    o_ref[0] = _dot3(s, w_ref[0]) + b_ref[0]


def _ada(s_rows, ada_w, ada_b):
    depth, d, n3 = ada_w.shape
    ct = 1024
    return pl.pallas_call(
        _ada_kernel,
        grid=(depth, n3 // ct),
        in_specs=[
            pl.BlockSpec((8, d), lambda l, j: (0, 0)),
            pl.BlockSpec((1, d, ct), lambda l, j: (l, 0, j)),
            pl.BlockSpec((1, 1, ct), lambda l, j: (l, 0, j)),
        ],
        out_specs=pl.BlockSpec((1, 8, ct), lambda l, j: (l, 0, j)),
        out_shape=jax.ShapeDtypeStruct((depth, 8, n3), F32),
        compiler_params=_cp(("arbitrary", "arbitrary")),
        name="ada_modulation",
    )(s_rows, ada_w, ada_b.reshape(depth, 1, n3))


def _inproj_kernel(*refs, row_major):
    if row_major:
        x_ref, ctx_ref, *refs = refs
    else:
        x_ref, *refs = refs
    (mod_ref, nw_ref, w_ref, cw_ref, par_ref, ml_ref, mr_ref,
     zs_ref, xbc_ref, da_ref, cc_ref, g0_ref, u_ref, cwb_ref) = refs
    t = pl.program_id(1)
    rc = 256

    @pl.when(t == 0)
    def _():
        for r in range(0, CONV_ROWS, 512):
            for k in range(4):
                cwb_ref[k, r:r + 512, :] = jnp.broadcast_to(cw_ref[r:r + 512, k:k + 1], (512, LANES))

    @pl.when(t < ALL_TILES)
    def _():
        if row_major:
            x = jnp.where(t < LAT_TILES, x_ref[0], ctx_ref[0])
            ms = jnp.mean(x * x, axis=1, keepdims=True)
            xn = x * lax.rsqrt(ms + EPS)
            xb = ((xn * nw_ref[...]) * mod_ref[0, 0, 1:2, :] + mod_ref[0, 0, 0:1, :]).astype(BF)
            dims = (((1,), (1,)), ((), ()))
        else:
            x = x_ref[0]
            ms = jnp.mean(x * x, axis=0, keepdims=True)
            xn = x * lax.rsqrt(ms + EPS)
            xb = ((xn * nw_ref[...]) * mod_ref[0, 0, :, 1:2] + mod_ref[0, 0, :, 0:1]).astype(BF)
            dims = (((1,), (0,)), ((), ()))
        ml = ml_ref[...]
        mr = mr_ref[...]

        def mm(r0, rows):
            return lax.dot_general(w_ref[r0:r0 + rows, :], xb, dims, preferred_element_type=F32)

        def conv(p, r0, rows):
            left = pltpu.roll(p, 1, axis=1) * ml
            right = pltpu.roll(p, TT - 1, axis=1) * mr
            rs = slice(r0, r0 + rows)
            halves = []
            for hl in range(0, TT, LANES):
                ls = slice(hl, hl + LANES)
                halves.append(cwb_ref[0, rs, :] * left[:, ls] + cwb_ref[1, rs, :] * p[:, ls]
                              + cwb_ref[2, rs, :] * right[:, ls] + cwb_ref[3, rs, :])
            return jnp.concatenate(halves, axis=1)

        for r in range(0, SSD_W, rc):
            zs_ref[0, r:r + rc, :] = _silu(mm(R_ZS + r, rc)).astype(zs_ref.dtype)
        for r in range(0, XBC_W, rc):
            xbc_ref[0, r:r + rc, :] = _silu(conv(mm(R_XBC + r, rc), r, rc)).astype(xbc_ref.dtype)
        nh = SSD_HEADS
        dt = jax.nn.softplus(mm(R_DT, DT_ROWS)[0:2 * nh, :] + par_ref[:, 0:1])
        a = dt * -jnp.exp(par_ref[:, 1:2])
        pos = lax.broadcasted_iota(jnp.int32, a.shape, 1) & (CHUNK - 1)
        pre = a
        suf = a
        k = 1
        while k < CHUNK:
            pre = pre + jnp.where(pos >= k, pltpu.roll(pre, k, axis=1), 0.0)
            suf = suf + jnp.where(pos < CHUNK - k, pltpu.roll(suf, TT - k, axis=1), 0.0)
            k *= 2
        tot = pre + suf - a
        cum = jnp.concatenate([pre[0:nh], suf[nh:2 * nh]], axis=0)
        da_ref[0] = jnp.concatenate([dt, cum, jnp.exp(tot - cum), jnp.exp(cum), jnp.exp(tot)], axis=0)
        for hc in range(TT // CHUNK):
            sq = jnp.concatenate([cum[:, hc * CHUNK:(hc + 1) * CHUNK],
                                  jnp.zeros((LANES - 2 * nh, CHUNK), F32)], axis=0)
            cc_ref[0, hc * CHUNK:(hc + 1) * CHUNK, :] = sq.T
        for r in range(0, HY_W, rc):
            gate = _silu(mm(R_ZH + r, rc)) * conv(mm(R_HY + r, rc), XBC_W + r, rc)
            g0_ref[0, r:r + rc, :] = gate.astype(g0_ref.dtype)
            p1 = conv(mm(R_HY + HY_W + r, rc), XBC_W + HY_W + r, rc)
            pv = conv(mm(R_HY + 2 * HY_W + r, rc), XBC_W + 2 * HY_W + r, rc)
            u_ref[0, r:r + rc, :] = (p1 * pv).astype(u_ref.dtype)

    @pl.when(t >= ALL_TILES)
    def _():
        u_ref[...] = jnp.zeros_like(u_ref)


def _mod_kind(i, t):
    return (i, jnp.where(t >= LAT_TILES, 1, 0), 0, 0)


def _stream_specs(row_major):
    if row_major:
        return [pl.BlockSpec((1, TT, D_MODEL), lambda i, t: (i, jnp.minimum(t, LAT_TILES - 1), 0)),
                pl.BlockSpec((1, CTX_LEN, D_MODEL), lambda i, t: (i, 0, 0))]
    return [pl.BlockSpec((1, D_MODEL, TT), lambda i, t: (i, 0, jnp.minimum(t, ALL_TILES - 1)))]


def _inproj(stream, mod, nw, w_t, layer, cw, par, ml, mr, *, row_major):
    b = stream[0].shape[0]
    vec_specs = [pl.BlockSpec((1, 1) + mod.shape[2:], _mod_kind), pl.BlockSpec(nw.shape, lambda i, t: (0, 0))]
    last = ALL_TILES - 1
    tok = lambda i, t: (i, 0, jnp.minimum(t, last))
    const2 = lambda i, t: (0, 0)
    out_tok = lambda rows: pl.BlockSpec((1, rows, TT), tok)
    return pl.pallas_call(
        functools.partial(_inproj_kernel, row_major=row_major),
        grid=(b, HY_TILES),
        in_specs=_stream_specs(row_major) + vec_specs + [
            pl.BlockSpec((None, R_END, D_MODEL), lambda i, t: (layer, 0, 0), pipeline_mode=pl.Buffered(1)),
            pl.BlockSpec((CONV_ROWS, 4), const2),
            pl.BlockSpec((2 * SSD_HEADS, 2), const2),
            pl.BlockSpec((1, TT), lambda i, t: (0, jnp.minimum(t, last))),
            pl.BlockSpec((1, TT), lambda i, t: (0, jnp.minimum(t, last))),
        ],
        out_specs=[
            out_tok(SSD_W), out_tok(XBC_W), out_tok(DA_ROWS),
            pl.BlockSpec((1, TT, LANES), lambda i, t: (i, jnp.minimum(t, last), 0)),
            out_tok(HY_W),
            pl.BlockSpec((1, HY_W, TT), lambda i, t: (i, 0, t)),
        ],
        out_shape=[
            jax.ShapeDtypeStruct((b, SSD_W, T_ALL), ACT),
            jax.ShapeDtypeStruct((b, XBC_W, T_ALL), ACT),
            jax.ShapeDtypeStruct((b, DA_ROWS, T_ALL), F32),
            jax.ShapeDtypeStruct((b, T_ALL, LANES), F32),
            jax.ShapeDtypeStruct((b, HY_W, T_ALL), ACT),
            jax.ShapeDtypeStruct((b, HY_W, T_HY), ACT),
        ],
        scratch_shapes=[pltpu.VMEM((4, CONV_ROWS, LANES), F32)],
        compiler_params=_cp(("parallel", "arbitrary")),
        name="in_projection",
    )(*stream, mod, nw, w_t, cw, par, ml, mr)


def _ssd_direction(d, ck, x_ref, da_ref, cc_ref, dexp_ref, y_ref, st_ref, xdt_ref, w_ref, valid):
    nh = SSD_HEADS
    hd = SSD_HEADDIM
    hpg = nh // SSD_GROUPS
    o = d * nh
    tk = slice(ck * CHUNK, (ck + 1) * CHUNK)
    field = lambda k: da_ref[0, 2 * k * nh + o:2 * k * nh + o + nh, tk]
    dt, cum_t, e_end, e_in, e_tot = (field(k) for k in range(5))
    cum_c = cc_ref[0, tk, o:o + nh]
    per_head = lambda v, h: jnp.broadcast_to(v[h:h + 1, :], (hd, CHUNK)).astype(BF)
    for h in range(nh):
        rows = slice(h * hd, (h + 1) * hd)
        xdt = x_ref[0, rows, tk].astype(BF) * per_head(dt, h)
        xdt_ref[rows, :] = xdt
        w_ref[rows, :] = xdt * per_head(e_end, h)
    for g in range(SSD_GROUPS):
        grows = slice(g * hpg * hd, (g + 1) * hpg * hd)
        b_t = x_ref[0, SSD_W + g * SSD_STATE:SSD_W + (g + 1) * SSD_STATE, tk].astype(BF)
        c_t = x_ref[0, SSD_W + (SSD_GROUPS + g) * SSD_STATE:
                    SSD_W + (SSD_GROUPS + g + 1) * SSD_STATE, tk].astype(BF)
        cb_t = lax.dot_general(b_t, c_t, (((0,), (0,)), ((), ())), preferred_element_type=F32).astype(BF)
        inter = jnp.dot(st_ref[grows, :].astype(BF), c_t, preferred_element_type=F32)
        s_new = lax.dot_general(w_ref[grows, :], b_t, (((1,), (1,)), ((), ())),
                                preferred_element_type=F32)
        for h in range(g * hpg, (g + 1) * hpg):
            rows = slice(h * hd, (h + 1) * hd)
            loc = slice((h - g * hpg) * hd, (h - g * hpg + 1) * hd)
            seg = cum_t[h:h + 1, :] - cum_c[:, h:h + 1]
            m_t = cb_t * jnp.exp(jnp.where(valid, seg, -jnp.inf)).astype(BF)
            intra = jnp.dot(xdt_ref[rows, :], m_t, preferred_element_type=F32)
            y_ref[0, rows, tk] = (intra + inter[loc] * e_in[h:h + 1, :]
                                  + dexp_ref[d, rows, :] * x_ref[0, rows, tk].astype(F32)).astype(y_ref.dtype)
            st_ref[rows, :] = e_tot[h:h + 1, :] * st_ref[rows, :] + s_new[loc]


def _ssd_kernel(xf_ref, xb_ref, daf_ref, dab_ref, ccf_ref, ccb_ref, dexp_ref, yf_ref, yb_ref,
                sf_ref, sb_ref, xdtf_ref, wf_ref, xdtb_ref, wb_ref):
    @pl.when(pl.program_id(1) == 0)
    def _():
        sf_ref[...] = jnp.zeros_like(sf_ref)
        sb_ref[...] = jnp.zeros_like(sb_ref)

    row = lax.broadcasted_iota(jnp.int32, (CHUNK, CHUNK), 0)
    col = lax.broadcasted_iota(jnp.int32, (CHUNK, CHUNK), 1)
    per = SSD_BLOCK // CHUNK
    for k in range(per):
        _ssd_direction(0, k, xf_ref, daf_ref, ccf_ref, dexp_ref, yf_ref, sf_ref, xdtf_ref, wf_ref,
                       row <= col)
        _ssd_direction(1, per - 1 - k, xb_ref, dab_ref, ccb_ref, dexp_ref, yb_ref, sb_ref, xdtb_ref, wb_ref,
                       row >= col)


def _ssd(xbc_t, da_t, cc, dexp):
    b = xbc_t.shape[0]
    nb = T_ALL // SSD_BLOCK
    lat = SEQ // SSD_BLOCK
    fwd = lambda i, s: (i, 0, (s + lat) % nb)
    bwd = lambda i, s: (i, 0, nb - 1 - s)
    fwd_r = lambda i, s: (i, (s + lat) % nb, 0)
    bwd_r = lambda i, s: (i, nb - 1 - s, 0)
    return pl.pallas_call(
        _ssd_kernel,
        grid=(b, nb),
        in_specs=[
            pl.BlockSpec((1, XBC_W, SSD_BLOCK), fwd),
            pl.BlockSpec((1, XBC_W, SSD_BLOCK), bwd),
            pl.BlockSpec((1, DA_ROWS, SSD_BLOCK), fwd),
            pl.BlockSpec((1, DA_ROWS, SSD_BLOCK), bwd),
            pl.BlockSpec((1, SSD_BLOCK, LANES), fwd_r),
            pl.BlockSpec((1, SSD_BLOCK, LANES), bwd_r),
            pl.BlockSpec((2, SSD_W, CHUNK), lambda i, s: (0, 0, 0)),
        ],
        out_specs=[pl.BlockSpec((1, SSD_W, SSD_BLOCK), fwd), pl.BlockSpec((1, SSD_W, SSD_BLOCK), bwd)],
        out_shape=[jax.ShapeDtypeStruct((b, SSD_W, T_ALL), ACT)] * 2,
        scratch_shapes=[pltpu.VMEM((SSD_W, SSD_STATE), F32)] * 2 + [pltpu.VMEM((SSD_W, CHUNK), BF)] * 4,
        compiler_params=_cp(("parallel", "arbitrary")),
        name="ssd_scan",
    )(xbc_t, xbc_t, da_t, da_t, cc, cc, dexp)


def _split_bf16(v):
    hi = v.astype(BF)
    return hi, (v - hi.astype(F32)).astype(BF)


def _dot3(a, b):
    a_hi, a_lo = _split_bf16(a)
    b_hi, b_lo = _split_bf16(b)
    return jnp.dot(jnp.concatenate([a_hi, a_hi, a_lo], axis=1),
                   jnp.concatenate([b_hi, b_lo, b_hi], axis=0), preferred_element_type=F32)


def _filt_hidden_kernel(f_ref, w1_ref, w2_ref, w3_ref, bf_ref, o_ref):
    fr = bf_ref[:, 3:4]
    hcur = jnp.sin(fr * (_dot3(w1_ref[...], f_ref[...]) + bf_ref[:, 0:1]))
    hcur = jnp.sin(fr * (_dot3(w2_ref[...], hcur) + bf_ref[:, 1:2]))
    o_ref[...] = jnp.sin(fr * (_dot3(w3_ref[...], hcur) + bf_ref[:, 2:3]))


def _filt_kernel(h_ref, w4_ref, pos_ref, dl_ref, o_ref, *, both):
    hid = h_ref[...]
    window = jnp.exp(-dl_ref[:, 0:1] * pos_ref[0:1, :])
    if both:
        filt = _dot3(w4_ref[0], hid) * pos_ref[1:2, :] + _dot3(w4_ref[1], hid) * pos_ref[2:3, :]
    else:
        filt = _dot3(w4_ref[0], hid) * (pos_ref[1:2, :] + pos_ref[2:3, :])
    filt = filt * window + dl_ref[:, 1:2] * pos_ref[3:4, :]
    o_ref[...] = filt.reshape(o_ref.shape)


def _hyena_filter(feats, pos, length, w1t, w2t, w3t, bfr, w4t, decay_skip):
    n = feats.shape[1]
    nt = min(n, 2048)
    full = lambda shape: pl.BlockSpec(shape, lambda j: tuple(0 for _ in shape))
    nth = min(n, 4096)
    hid = pl.pallas_call(
        _filt_hidden_kernel,
        grid=(n // nth,),
        in_specs=[pl.BlockSpec((HY_FILT_W, nth), lambda j: (0, j)),
                  full((HY_FILT_W, HY_FILT_W)), full((HY_FILT_W, HY_FILT_W)),
                  full((HY_FILT_W, HY_FILT_W)), full((HY_FILT_W, 4))],
        out_specs=pl.BlockSpec((HY_FILT_W, nth), lambda j: (0, j)),
        out_shape=jax.ShapeDtypeStruct((HY_FILT_W, n), F32),
        compiler_params=_cp(("arbitrary",)),
        name="hyena_filter_hidden",
    )(feats, w1t, w2t, w3t, bfr)
    ct = 512
    tiles = n // nt
    both = tiles == 1
    assert both or n == 2 * length
    w4_spec = (pl.BlockSpec((2, ct, HY_FILT_W), lambda c, j: (0, c, 0)) if both else
               pl.BlockSpec((1, ct, HY_FILT_W), lambda c, j: (j // (tiles // 2), c, 0)))
    return pl.pallas_call(
        functools.partial(_filt_kernel, both=both),
        grid=(HY_W // ct, tiles),
        in_specs=[pl.BlockSpec((HY_FILT_W, nt), lambda c, j: (0, j)),
                  w4_spec,
                  pl.BlockSpec((8, nt), lambda c, j: (0, j)),
                  pl.BlockSpec((ct, 2), lambda c, j: (c, 0))],
        out_specs=pl.BlockSpec((ct, nt // LANES, LANES), lambda c, j: (c, j, 0)),
        out_shape=jax.ShapeDtypeStruct((HY_W, n // LANES, LANES), F32),
        compiler_params=_cp(("parallel", "arbitrary")),
        name="hyena_filter",
    )(hid, w4t, pos, decay_skip)


def _hyconv_kernel(u_ref, f_ref, g1_ref, g1f_ref, g1i_ref, tw_ref, w3_ref, w3c_ref,
                   o_ref, ub_ref, l3_ref, l3f_ref, hs_ref, bp_ref, *, n1, r, ct):
    twr = tw_ref[0].astype(BF)
    twi = tw_ref[1].astype(BF)
    for bb in range(2):
        ub_ref[bb] = u_ref[bb].astype(F32).reshape(ct, r, LANES)

    def twiddled_rows(dst_ref, col, c):
        cr, ci = col[:n1], col[n1:]
        dst_ref[c * n1:(c + 1) * n1, 0:LANES] = cr * twr - ci * twi
        dst_ref[c * n1:(c + 1) * n1, LANES:2 * LANES] = cr * twi + ci * twr

    def column_dfts(c):
        rhs = jnp.concatenate(
            [jnp.concatenate([ub_ref[0, c + j], ub_ref[1, c + j]], axis=0) for j in range(2)],
            axis=1).astype(BF)
        bcol = jnp.dot(g1_ref[...], rhs, preferred_element_type=F32).astype(BF)
        for j in range(2):
            twiddled_rows(l3_ref, bcol[:, j * LANES:(j + 1) * LANES], c + j)

    def filter_column_dfts(c):
        frhs = jnp.concatenate([f_ref[c], f_ref[c + 1]], axis=1).astype(BF)
        fcol = jnp.dot(g1f_ref[...], frhs, preferred_element_type=F32).astype(BF)
        for j in range(2):
            twiddled_rows(l3f_ref, fcol[:, j * LANES:(j + 1) * LANES], c + j)

    rows_b = 256

    def filter_spectrum(k):
        rows = slice(k * rows_b, (k + 1) * rows_b)
        hs_ref[rows, :] = jnp.dot(l3f_ref[rows, :], w3_ref[...],
                                  preferred_element_type=F32).astype(BF)

    def spectrum_product(k):
        rows = slice(k * rows_b, (k + 1) * rows_b)
        xs = jnp.dot(l3_ref[rows, :], w3_ref[...], preferred_element_type=F32).astype(BF)
        xr, xi = xs[:, :LANES], xs[:, LANES:]
        hr, hi = hs_ref[rows, 0:LANES], hs_ref[rows, LANES:2 * LANES]
        ys = jnp.concatenate([xr * hr - xi * hi, xr * hi + xi * hr], axis=1)
        bp_ref[rows, :] = jnp.dot(ys, w3c_ref[...], preferred_element_type=F32).astype(BF)

    def untwiddled(c):
        bp = bp_ref[c * n1:(c + 1) * n1, :]
        br, bi = bp[:, :LANES], bp[:, LANES:]
        return jnp.concatenate([br * twr + bi * twi, bi * twr - br * twi], axis=0)

    def inverse_column_dfts(c):
        rhs = jnp.concatenate([untwiddled(c), untwiddled(c + 1)], axis=1)
        out = jnp.dot(g1i_ref[...], rhs, preferred_element_type=F32)
        for j in range(2):
            ub_ref[0, c + j] = out[:r, j * LANES:(j + 1) * LANES]
            ub_ref[1, c + j] = out[r:, j * LANES:(j + 1) * LANES]

    for c in range(0, ct, 2):
        filter_column_dfts(c)
    for k in range(ct * n1 // rows_b):
        filter_spectrum(k)
    for c in range(0, ct, 2):
        column_dfts(c)
    for k in range(ct * n1 // rows_b):
        spectrum_product(k)
    for c in range(0, ct, 2):
        inverse_column_dfts(c)

    for bb in range(2):
        o_ref[bb] = ub_ref[bb].reshape(ct, r * LANES).astype(o_ref.dtype)


@functools.lru_cache(maxsize=None)
def _dft_constants(n1, r):
    n = n1 * LANES
    k1 = np.arange(n1)
    f1 = np.exp(-2j * np.pi * np.outer(k1, k1) / n1)
    f2 = np.exp(-2j * np.pi * np.outer(np.arange(LANES), np.arange(LANES)) / LANES)
    tw = np.exp(-2j * np.pi * np.outer(k1, np.arange(LANES)) / n)
    g1 = np.block([[f1.real[:, :r], -f1.imag[:, :r]], [f1.imag[:, :r], f1.real[:, :r]]])
    g1f = np.concatenate([f1.real, f1.imag], axis=0)
    g1i = np.block([[f1.real[:r], f1.imag[:r]], [-f1.imag[:r], f1.real[:r]]]) / n
    w3 = np.block([[f2.real, f2.imag], [-f2.imag, f2.real]])
    w3c = np.block([[f2.real, -f2.imag], [f2.imag, f2.real]])
    tws = np.stack([tw.real, tw.imag])
    return tuple(np.asarray(a, np.float32) for a in (g1, g1f, g1i, tws, w3, w3c))


def _hyconv(u_t, filt, *, n1, r, tok_block):
    b, c_all = u_t.shape[0], u_t.shape[1]
    ct = 32 if n1 >= 128 else 64
    g1, g1f, g1i, tws, w3, w3c = (jnp.asarray(a) for a in _dft_constants(n1, r))
    g1, g1f, g1i, w3, w3c = (a.astype(BF) for a in (g1, g1f, g1i, w3, w3c))
    sig = pl.BlockSpec((b, ct, r * LANES), lambda i: (0, i, tok_block))
    full = lambda a: pl.BlockSpec(a.shape, lambda i: tuple(0 for _ in a.shape))
    return pl.pallas_call(
        functools.partial(_hyconv_kernel, n1=n1, r=r, ct=ct),
        grid=(c_all // ct,),
        in_specs=[sig, pl.BlockSpec((ct, n1, LANES), lambda i: (i, 0, 0)),
                  full(g1), full(g1f), full(g1i), full(tws), full(w3), full(w3c)],
        out_specs=pl.BlockSpec((b, ct, r * LANES), lambda i: (0, i, 0)),
        out_shape=jax.ShapeDtypeStruct((b, c_all, r * LANES), ACT),
        scratch_shapes=[pltpu.VMEM((b, ct, r, LANES), F32),
                        pltpu.VMEM((ct * n1, 2 * LANES), BF), pltpu.VMEM((ct * n1, 2 * LANES), BF),
                        pltpu.VMEM((ct * n1, 2 * LANES), BF), pltpu.VMEM((ct * n1, 2 * LANES), BF)],
        compiler_params=_cp(("arbitrary",)),
        name="hyena_conv_n%d" % n1,
    )(u_t, filt, g1, g1f, g1i, tws, w3, w3c)


def _merge_kernel(*refs, final, row_major):
    yf_ref, yb_ref, zs_ref, g0_ref, yh_ref, *refs = refs
    if final:
        y_h = yh_ref[0]
    else:
        yhc_ref, *refs = refs
        y_h = jnp.where(pl.program_id(1) < LAT_TILES, yh_ref[0], yhc_ref[0])
    if row_major:
        x_ref, ctx_ref, mod_ref, gw_ref, w_ref, fw_ref, o_ref = refs
        x = jnp.where(pl.program_id(1) < LAT_TILES, x_ref[0], ctx_ref[0]).T
    else:
        x_ref, mod_ref, gw_ref, w_ref, fw_ref, o_ref = refs
        x = x_ref[0]
    gate = mod_ref[0, 0, :, 2:3]

    def projected_group(g, cols):
        rs = lax.rsqrt(jnp.mean(g * g, axis=0, keepdims=True) + EPS)
        scaled = (g * gw_ref[cols, :]).astype(BF)
        return jnp.dot(w_ref[:, cols], scaled, preferred_element_type=F32) * rs

    g_s = (yf_ref[0].astype(F32) + yb_ref[0].astype(F32)) * zs_ref[0].astype(F32)
    g_h = y_h.astype(F32) * g0_ref[0].astype(F32)
    out = projected_group(g_s, slice(0, SSD_W)) + projected_group(g_h, slice(SSD_W, SSD_W + HY_W))
    xn = x + gate * out
    if final:
        ms = jnp.mean(xn * xn, axis=0, keepdims=True)
        o_ref[0] = ((xn * lax.rsqrt(ms + EPS)) * fw_ref[...]).T
    else:
        o_ref[0] = xn


def _merge(yf, yb, zs, g0, yh, stream, mod, gw, w_out_t, layer, fw, *, final, row_major):
    b = yf.shape[0]
    tiles = LAT_TILES if final else ALL_TILES
    tok = lambda i, t: (i, 0, t)
    const2 = lambda i, t: (0, 0)
    act = lambda rows: pl.BlockSpec((1, rows, TT), tok)
    if final:
        out_spec = pl.BlockSpec((1, TT, D_MODEL), lambda i, t: (i, t, 0))
        out_shape = jax.ShapeDtypeStruct((b, SEQ, D_MODEL), F32)
    else:
        out_spec = act(D_MODEL)
        out_shape = jax.ShapeDtypeStruct((b, D_MODEL, T_ALL), F32)
    yh_specs = [pl.BlockSpec((1, HY_W, TT), lambda i, t: (i, 0, jnp.minimum(t, LAT_TILES - 1)))]
    if not final:
        yh_specs.append(pl.BlockSpec((1, HY_W, TT), lambda i, t: (i, 0, 0)))
    return pl.pallas_call(
        functools.partial(_merge_kernel, final=final, row_major=row_major),
        grid=(b, tiles),
        in_specs=[act(SSD_W), act(SSD_W), act(SSD_W), act(HY_W)] + yh_specs + _stream_specs(row_major) + [
                  pl.BlockSpec((1, 1, D_MODEL, 4), _mod_kind),
                  pl.BlockSpec((SSD_W + HY_W, 1), const2),
                  pl.BlockSpec((None, D_MODEL, SSD_W + HY_W), lambda i, t: (layer, 0, 0)),
                  pl.BlockSpec((D_MODEL, 1), const2)],
        out_specs=out_spec,
        out_shape=out_shape,
        compiler_params=_cp(("parallel", "arbitrary")),
        name="merge_out_projection",
    )(yf, yb, zs, g0, *yh, *stream, mod, gw, w_out_t, fw)


@functools.lru_cache(maxsize=None)
def _conv_masks():
    t = np.arange(T_ALL)
    pos = np.where(t < SEQ, t % GRID_W, t - SEQ)
    period = np.where(t < SEQ, GRID_W, CTX_LEN)
    ml = (pos != 0).astype(np.float32)[None, :]
    mr = (pos != period - 1).astype(np.float32)[None, :]
    return ml, mr


@functools.lru_cache(maxsize=None)
def _filter_positions(n, length):
    idx = np.arange(n)
    fwd = idx < length
    bwd = idx > n - length
    d = np.where(fwd, idx, np.where(bwd, n - idx, 0)).astype(np.float64)
    t = (np.linspace(0.0, 1.0, length, dtype=np.float32).astype(np.float64))[d.astype(np.int64)]
    w = (2.0 * math.pi / length) * d
    f = np.linspace(1e-4, HY_BANDS - 1, HY_BANDS, dtype=np.float32).astype(np.float64)[:, None]
    feats = np.zeros((HY_FILT_W, n), np.float32)
    feats[0] = t
    feats[1:1 + HY_BANDS] = np.cos(f * w[None, :])
    feats[1 + HY_BANDS:HY_EMB] = -np.sin(f * w[None, :])
    pos = np.zeros((8, n), np.float32)
    pos[0] = t
    pos[1] = fwd
    pos[2] = bwd
    pos[3] = idx == 0
    return feats, pos


def kernel(x, c, ctx, c_ctx, norm_w, ada_w, ada_b, w_in, ssd_conv_w, ssd_conv_b, dt_bias, a_log, d_skip,
           hy_conv_w, hy_conv_b, filt_w1, filt_b1, filt_w2, filt_b2, filt_w3, filt_b3, filt_w4, filt_freq,
           hy_bias, gnorm_w, w_out, final_norm_w):
    bsz = x.shape[0]
    depth = norm_w.shape[0]
    assert x.shape == (bsz, SEQ, D_MODEL) and ctx.shape == (bsz, CTX_LEN, D_MODEL) and bsz == 2

    s_rows = jnp.zeros((8, D_MODEL), F32).at[:bsz].set(c).at[bsz].set(c_ctx)
    mods = _ada(s_rows, ada_w, ada_b)
    mods = mods[:, :bsz + 1].reshape(depth, bsz + 1, 3, D_MODEL)
    lat = mods[:, :bsz]
    cx = jnp.broadcast_to(mods[:, bsz:bsz + 1], lat.shape)
    mod = jnp.stack([lat, cx], axis=2)
    mod = jnp.stack([mod[:, :, :, 0], 1.0 + mod[:, :, :, 1], mod[:, :, :, 2], jnp.zeros_like(mod[:, :, :, 0])],
                    axis=-1)
    mod_rows = jnp.swapaxes(mod, -1, -2)

    ml, mr = (jnp.asarray(a) for a in _conv_masks())
    deltas = jnp.abs(jnp.linspace(HY_MIN_DECAY, HY_MAX_DECAY, HY_W, dtype=F32))

    w_in_t = jnp.swapaxes(w_in, 1, 2).astype(BF)
    w_out_t = jnp.swapaxes(w_out, 1, 2).astype(BF)

    stream = (x, ctx)
    out = None
    for l in range(depth):
        last = l == depth - 1
        row_major = l == 0
        cw = jnp.concatenate([
            jnp.concatenate([ssd_conv_w[l].T, ssd_conv_b[l][:, None]], axis=1),
            jnp.concatenate([hy_conv_w[l].T, hy_conv_b[l][:, None]], axis=1)], axis=0)
        par = jnp.stack([dt_bias[l].reshape(-1), a_log[l].reshape(-1)], axis=1)
        if row_major:
            vecs = (mod_rows[l], norm_w[l][None, :])
        else:
            vecs = (mod[l], norm_w[l][:, None])
        zs, xbc, da, cc, g0, u = _inproj(stream, *vecs, w_in_t, l, cw, par, ml, mr, row_major=row_major)

        dexp = jnp.broadcast_to(jnp.repeat(d_skip[l], SSD_HEADDIM, axis=1)[:, :, None],
                                (2, SSD_W, CHUNK))
        yf, yb = _ssd(xbc, da, cc, dexp)

        w4t = filt_w4[l].T.reshape(2, HY_W, HY_FILT_W)
        bfr = jnp.stack([filt_b1[l], filt_b2[l], filt_b3[l], filt_freq[l]], axis=1)
        w1t = jnp.pad(filt_w1[l].T, ((0, 0), (0, HY_FILT_W - HY_EMB)))
        decay_skip = jnp.stack([deltas, hy_bias[l]], axis=1)

        def long_conv(n1, r, tok_block, length):
            feats, pos = (jnp.asarray(a) for a in _filter_positions(n1 * LANES, length))
            filt = _hyena_filter(feats, pos, length, w1t, filt_w2[l].T, filt_w3[l].T, bfr, w4t, decay_skip)
            return _hyconv(u, filt, n1=n1, r=r, tok_block=tok_block)

        yh = (long_conv(2 * SEQ // LANES, SEQ // LANES, 0, SEQ),)
        if not last:
            yh += (long_conv(16, HY_CTX_ROWS, SEQ // (HY_CTX_ROWS * LANES), CTX_LEN),)

        res = _merge(yf, yb, zs, g0, yh, stream, mod[l], gnorm_w[l][:, None], w_out_t, l,
                     final_norm_w[:, None], final=last, row_major=row_major)
        if last:
            out = res
        else:
            stream = (res,)
    return out
```
